```python
import jax, jax.numpy as jnp
from jax import lax
import numpy as np

D_MODEL = 1024
BATCH = 16
SEQ = 2048
DEPTH = 4
DEC_BATCH = 1
DEC_SEQ = 16384
PAST_LEN = 128

N_MIXERS = 3
GRID_W = 64
NORM_EPS = 1e-6
CHUNK = 128
A_WIDTH = 3 * D_MODEL
A_GROUPS = 8
A_GROUP_DIM = A_WIDTH // A_GROUPS
HEAD_DIM = 128
N_HEADS = D_MODEL // HEAD_DIM
N_KV_HEADS = 2
Q_PER_KV = N_HEADS // N_KV_HEADS
AXIS_DIM = HEAD_DIM // 2
ROPE_THETA = 10000.0
Q_BLOCK = 128
C_GROUPS = 8
C_GROUP_DIM = D_MODEL // C_GROUPS
D_FF = 4 * D_MODEL

kernel_name = "hybrid_interleaved_gmlp_gqa_fnet_encoder"


def rms_norm(x, g):
    xf = x.astype(jnp.float32)
    y = xf * lax.rsqrt(jnp.mean(xf * xf, axis=-1, keepdims=True) + NORM_EPS)
    return (y * g.astype(jnp.float32)).astype(x.dtype)


def gmlp_mixer(h, w_in, v_norm, w_s, b_s, w_out):
    b, s, _ = h.shape
    z = jax.nn.gelu(h @ w_in, approximate=False)
    u, v = jnp.split(z, 2, axis=-1)
    v = rms_norm(v, v_norm)
    v = v.reshape(b, s // CHUNK, CHUNK, A_GROUPS, A_GROUP_DIM)
    sv = jnp.einsum('gpq,bcqgd->bcpgd', w_s, v) + b_s.T[:, :, None]
    return (u * sv.reshape(b, s, A_WIDTH)) @ w_out


def rope_section(xs, ang):
    x1, x2 = jnp.split(xs.astype(jnp.float32), 2, axis=-1)
    cos = jnp.cos(ang)[None, :, None, :]
    sin = jnp.sin(ang)[None, :, None, :]
    return jnp.concatenate([x1 * cos - x2 * sin, x1 * sin + x2 * cos], axis=-1)


def axial_rope(x, ang_row, ang_col):
    out = jnp.concatenate([rope_section(x[..., :AXIS_DIM], ang_row),
                           rope_section(x[..., AXIS_DIM:], ang_col)], axis=-1)
    return out.astype(x.dtype)


def gqa_mixer(h, w_qkv, q_norm, k_norm, w_o, ang_row, ang_col):
    b, s, _ = h.shape
    qkv = h @ w_qkv
    q = qkv[..., :N_HEADS * HEAD_DIM].reshape(b, s, N_HEADS, HEAD_DIM)
    k = qkv[..., N_HEADS * HEAD_DIM:(N_HEADS + N_KV_HEADS) * HEAD_DIM].reshape(b, s, N_KV_HEADS, HEAD_DIM)
    v = qkv[..., (N_HEADS + N_KV_HEADS) * HEAD_DIM:].reshape(b, s, N_KV_HEADS, HEAD_DIM)
    q = axial_rope(rms_norm(q, q_norm), ang_row, ang_col)
    k = axial_rope(rms_norm(k, k_norm), ang_row, ang_col)
    scale = HEAD_DIM ** -0.5
    k = k.transpose(0, 2, 1, 3)
    v = v.transpose(0, 2, 1, 3)
    nb = s // Q_BLOCK
    qb = q.reshape(b, nb, Q_BLOCK, N_KV_HEADS, Q_PER_KV, HEAD_DIM).transpose(1, 0, 3, 4, 2, 5)

    def block(qblk):
        sc = jnp.einsum('bkgqd,bksd->bkgqs', qblk, k).astype(jnp.float32) * scale
        p = jax.nn.softmax(sc, axis=-1)
        return jnp.einsum('bkgqs,bksd->bkgqd', p.astype(v.dtype), v)

    o = lax.map(block, qb)
    o = o.transpose(1, 0, 4, 2, 3, 5).reshape(b, s, N_HEADS * HEAD_DIM)
    return o @ w_o


def fnet_mixer(h, w_out):
    b, s, _ = h.shape
    hg = h.astype(jnp.float32).reshape(b, s, C_GROUPS, C_GROUP_DIM)
    f = jnp.fft.fftn(hg, axes=(1, 3), norm='ortho').real
    return f.astype(h.dtype).reshape(b, s, D_MODEL) @ w_out


def sq_relu_mlp(h, w_in, w_out):
    return jnp.square(jax.nn.relu(h @ w_in)) @ w_out


def trunk(x, norm_gains, a_w_in, a_v_norm, a_w_s, a_b_s, a_w_out,
          b_w_qkv, b_q_norm, b_k_norm, b_w_o, c_w_out, mlp_w_in, mlp_w_out):
    s = x.shape[1]
    rows = s // GRID_W
    row_pos = jnp.repeat(jnp.arange(rows, dtype=jnp.float32), GRID_W)
    col_pos = jnp.tile(jnp.arange(GRID_W, dtype=jnp.float32), rows)
    inv_freq = ROPE_THETA ** (-jnp.arange(0, AXIS_DIM, 2, dtype=jnp.float32) / AXIS_DIM)
    ang_row = row_pos[:, None] * inv_freq[None, :]
    ang_col = col_pos[:, None] * inv_freq[None, :]
    for i in range(DEPTH):
        m, j = i % N_MIXERS, i // N_MIXERS
        g = norm_gains[i]
        h = rms_norm(x, g[0])
        if m == 0:
            y = gmlp_mixer(h, a_w_in[j], a_v_norm[j], a_w_s[j], a_b_s[j], a_w_out[j])
        elif m == 1:
            y = gqa_mixer(h, b_w_qkv[j], b_q_norm[j], b_k_norm[j], b_w_o[j], ang_row, ang_col)
        else:
            y = fnet_mixer(h, c_w_out[j])
        x = x + rms_norm(y, g[1])
        h = rms_norm(x, g[2])
        x = x + rms_norm(sq_relu_mlp(h, mlp_w_in[i], mlp_w_out[i]), g[3])
    return x


def setup_inputs(seed: int = 0) -> dict:
    key = jax.random.key(seed)
    ks = jax.random.split(key, 16)
    n_a = (DEPTH + 2) // 3
    n_b = (DEPTH + 1) // 3
    n_c = DEPTH // 3
    nrm = lambda k, shape, fan_in: jax.random.normal(k, shape, jnp.float32) * fan_in ** -0.5
    qkv_out = (N_HEADS + 2 * N_KV_HEADS) * HEAD_DIM
    return {
        "x_prompt": jax.random.normal(ks[0], (BATCH, SEQ, D_MODEL), jnp.float32),
        "x_sample": jax.random.normal(ks[1], (DEC_BATCH, DEC_SEQ, D_MODEL), jnp.float32),
        "norm_gains": 1.0 + 0.02 * jax.random.normal(ks[2], (DEPTH, 4, D_MODEL), jnp.float32),
        "a_w_in": nrm(ks[3], (n_a, D_MODEL, 2 * A_WIDTH), D_MODEL),
        "a_v_norm": 1.0 + 0.02 * jax.random.normal(ks[4], (n_a, A_WIDTH), jnp.float32),
        "a_w_s": nrm(ks[5], (n_a, A_GROUPS, CHUNK, CHUNK), CHUNK),
        "a_b_s": 1.0 + 0.02 * jax.random.normal(ks[6], (n_a, A_GROUPS, CHUNK), jnp.float32),
        "a_w_out": nrm(ks[7], (n_a, A_WIDTH, D_MODEL), A_WIDTH),
        "b_w_qkv": nrm(ks[8], (n_b, D_MODEL, qkv_out), D_MODEL),
        "b_q_norm": 1.0 + 0.02 * jax.random.normal(ks[9], (n_b, HEAD_DIM), jnp.float32),
        "b_k_norm": 1.0 + 0.02 * jax.random.normal(ks[10], (n_b, HEAD_DIM), jnp.float32),
        "b_w_o": nrm(ks[11], (n_b, N_HEADS * HEAD_DIM, D_MODEL), N_HEADS * HEAD_DIM),
        "c_w_out": nrm(ks[12], (n_c, D_MODEL, D_MODEL), D_MODEL),
        "mlp_w_in": nrm(ks[13], (DEPTH, D_MODEL, D_FF), D_MODEL),
        "mlp_w_out": nrm(ks[14], (DEPTH, D_FF, D_MODEL), D_FF),
    }


def reference(x_prompt, x_sample, norm_gains, a_w_in, a_v_norm, a_w_s, a_b_s, a_w_out,
              b_w_qkv, b_q_norm, b_k_norm, b_w_o, c_w_out, mlp_w_in, mlp_w_out):
    y_prompt = trunk(x_prompt, norm_gains, a_w_in, a_v_norm, a_w_s, a_b_s, a_w_out,
                     b_w_qkv, b_q_norm, b_k_norm, b_w_o, c_w_out, mlp_w_in, mlp_w_out)
    y_sample = trunk(x_sample, norm_gains, a_w_in, a_v_norm, a_w_s, a_b_s, a_w_out,
                     b_w_qkv, b_q_norm, b_k_norm, b_w_o, c_w_out, mlp_w_in, mlp_w_out)
    return (y_prompt, y_sample)
```

```python
import functools
import math

import jax
import jax.numpy as jnp
from jax import lax
from jax.experimental import pallas as pl
from jax.experimental.pallas import tpu as pltpu

D_MODEL = 1024
DEPTH = 4
N_MIXERS = 3
GRID_W = 64
NORM_EPS = 1e-6
CHUNK = 128
A_WIDTH = 3 * D_MODEL
A_GROUPS = 8
A_GROUP_DIM = A_WIDTH // A_GROUPS
HEAD_DIM = 128
N_HEADS = D_MODEL // HEAD_DIM
N_KV_HEADS = 2
Q_PER_KV = N_HEADS // N_KV_HEADS
AXIS_DIM = HEAD_DIM // 2
ROPE_THETA = 10000.0
C_GROUPS = 8
C_GROUP_DIM = D_MODEL // C_GROUPS
D_FF = 4 * D_MODEL

SUBLANES = 8
VMEM_LIMIT_BYTES = 56 * 1024 * 1024

DFT_N2 = 128

BF16 = jnp.bfloat16
F32 = jnp.float32


def _rms(x, g):
    return x * lax.rsqrt(jnp.mean(x * x, axis=-1, keepdims=True) + NORM_EPS) * g


def _gelu(x):
    return 0.5 * x * (1.0 + lax.erf(x * math.sqrt(0.5)))


def _dot(a, b):
    return jnp.dot(a, b, preferred_element_type=F32)


def _const_spec(shape):
    nd = len(shape)
    return pl.BlockSpec(shape, lambda *_: (0,) * nd, pipeline_mode=pl.Buffered(1))


def _params(n_axes):
    return pltpu.CompilerParams(dimension_semantics=("parallel",) * n_axes,
                                vmem_limit_bytes=VMEM_LIMIT_BYTES)


def _mlp_kernel(x_ref, g_ref, w1_ref, w2_ref, o_ref, *, ff_chunk):
    x = x_ref[...]
    h = _rms(x, g_ref[2:3, :]).astype(BF16)
    acc = jnp.zeros(x.shape, F32)
    for c in range(D_FF // ff_chunk):
        cols = slice(c * ff_chunk, (c + 1) * ff_chunk)
        a = _dot(h, w1_ref[:, cols])
        a = jnp.square(jnp.maximum(a, 0.0)).astype(BF16)
        acc = acc + _dot(a, w2_ref[cols, :])
    o_ref[...] = x + _rms(acc, g_ref[3:4, :])


def _mlp_call(x, gains, w1, w2, *, tm=512, ff_chunk=1024):
    t, d = x.shape
    row_spec = pl.BlockSpec((tm, d), lambda i: (i, 0))
    return pl.pallas_call(
        functools.partial(_mlp_kernel, ff_chunk=ff_chunk),
        grid=(t // tm,),
        in_specs=[row_spec, _const_spec(gains.shape), _const_spec(w1.shape), _const_spec(w2.shape)],
        out_specs=row_spec,
        out_shape=jax.ShapeDtypeStruct(x.shape, x.dtype),
        compiler_params=_params(1),
        name="mlp",
    )(x, gains, w1, w2)


def _gmlp_kernel(x_ref, g_ref, w_in_ref, vn_ref, ws_ref, bs_ref, w_out_ref, o_ref, v_scr):
    x = x_ref[...]
    tm = x.shape[0]
    h = _rms(x, g_ref[0:1, :]).astype(BF16)
    ss = jnp.zeros((tm, 1), F32)
    for g in range(A_GROUPS):
        cols = slice(g * A_GROUP_DIM, (g + 1) * A_GROUP_DIM)
        vg = _gelu(_dot(h, w_in_ref[:, A_WIDTH + g * A_GROUP_DIM:A_WIDTH + (g + 1) * A_GROUP_DIM]))
        ss = ss + jnp.sum(vg * vg, axis=-1, keepdims=True)
        v_scr[:, cols] = vg
    r = lax.rsqrt(ss * (1.0 / A_WIDTH) + NORM_EPS)
    acc = jnp.zeros(x.shape, F32)
    for g in range(A_GROUPS):
        cols = slice(g * A_GROUP_DIM, (g + 1) * A_GROUP_DIM)
        vn = (v_scr[:, cols] * r * vn_ref[:, cols]).astype(BF16)
        u = _gelu(_dot(h, w_in_ref[:, cols]))
        ws = ws_ref[g]
        bias = bs_ref[:, g:g + 1]
        gated = []
        for c in range(tm // CHUNK):
            rows = slice(c * CHUNK, (c + 1) * CHUNK)
            sv = _dot(ws, vn[rows, :]) + bias
            gated.append((u[rows, :] * sv).astype(BF16))
        acc = acc + _dot(jnp.concatenate(gated, axis=0), w_out_ref[cols, :])
    o_ref[...] = x + _rms(acc, g_ref[1:2, :])


def _gmlp_call(x, gains, w_in, v_norm, w_s, b_s_t, w_out, *, tm=512):
    t, d = x.shape
    row_spec = pl.BlockSpec((tm, d), lambda i: (i, 0))
    return pl.pallas_call(
        _gmlp_kernel,
        grid=(t // tm,),
        in_specs=[row_spec, _const_spec(gains.shape), _const_spec(w_in.shape), _const_spec(v_norm.shape),
                  _const_spec(w_s.shape), _const_spec(b_s_t.shape), _const_spec(w_out.shape)],
        out_specs=row_spec,
        out_shape=jax.ShapeDtypeStruct(x.shape, x.dtype),
        scratch_shapes=[pltpu.VMEM((tm, A_WIDTH), F32)],
        compiler_params=_params(1),
        name="gmlp",
    )(x, gains, w_in, v_norm, w_s, b_s_t, w_out)


def _rope_tables(s):
    pos = jnp.arange(s, dtype=jnp.int32)
    row_pos = (pos // GRID_W).astype(F32)
    col_pos = (pos % GRID_W).astype(F32)
    inv_freq = ROPE_THETA ** (-jnp.arange(0, AXIS_DIM, 2, dtype=F32) / AXIS_DIM)
    ang_row = row_pos[:, None] * inv_freq[None, :]
    ang_col = col_pos[:, None] * inv_freq[None, :]
    cos = jnp.concatenate([jnp.cos(ang_row)] * 2 + [jnp.cos(ang_col)] * 2, axis=-1)
    sin = jnp.concatenate([-jnp.sin(ang_row), jnp.sin(ang_row), -jnp.sin(ang_col), jnp.sin(ang_col)], axis=-1)
    return cos, sin


def _qkv_kernel(x_ref, g_ref, w_ref, qn_ref, kn_ref, cos_ref, sin_ref, q_ref, k_ref, v_ref):
    h = _rms(x_ref[...], g_ref[0:1, :]).astype(BF16)
    qkv = _dot(h, w_ref[...])
    cos = cos_ref[...]
    sin = sin_ref[...]
    half = AXIS_DIM // 2
    lane = lax.broadcasted_iota(jnp.int32, cos.shape, 1)
    first_half = (lane % AXIS_DIM) < half
    scale = HEAD_DIM ** -0.5

    def norm_rope(t, gain):
        t = _rms(t, gain)
        partner = jnp.where(first_half, pltpu.roll(t, HEAD_DIM - half, 1), pltpu.roll(t, half, 1))
        return t * cos + partner * sin

    for hd in range(N_HEADS):
        cols = slice(hd * HEAD_DIM, (hd + 1) * HEAD_DIM)
        q_ref[:, cols] = (norm_rope(qkv[:, cols], qn_ref[...]) * scale).astype(BF16)
    for hd in range(N_KV_HEADS):
        cols = slice(hd * HEAD_DIM, (hd + 1) * HEAD_DIM)
        kcols = slice((N_HEADS + hd) * HEAD_DIM, (N_HEADS + hd + 1) * HEAD_DIM)
        vcols = slice((N_HEADS + N_KV_HEADS + hd) * HEAD_DIM, (N_HEADS + N_KV_HEADS + hd + 1) * HEAD_DIM)
        k_ref[:, cols] = norm_rope(qkv[:, kcols], kn_ref[...]).astype(BF16)
        v_ref[:, cols] = qkv[:, vcols].astype(BF16)


def _qkv_call(x, gains, w_qkv, q_norm, k_norm, cos, sin, *, tm=512):
    t, d = x.shape
    s = cos.shape[0]
    blocks_per_seq = s // tm
    row = lambda i: (i, 0)
    pos = lambda i: (i % blocks_per_seq, 0)
    kv_w = N_KV_HEADS * HEAD_DIM
    return pl.pallas_call(
        _qkv_kernel,
        grid=(t // tm,),
        in_specs=[pl.BlockSpec((tm, d), row), _const_spec(gains.shape), _const_spec(w_qkv.shape),
                  _const_spec(q_norm.shape), _const_spec(k_norm.shape),
                  pl.BlockSpec((tm, HEAD_DIM), pos), pl.BlockSpec((tm, HEAD_DIM), pos)],
        out_specs=[pl.BlockSpec((tm, d), row), pl.BlockSpec((tm, kv_w), row), pl.BlockSpec((tm, kv_w), row)],
        out_shape=[jax.ShapeDtypeStruct((t, d), BF16), jax.ShapeDtypeStruct((t, kv_w), BF16),
                   jax.ShapeDtypeStruct((t, kv_w), BF16)],
        compiler_params=_params(1),
        name="qkv",
    )(x, gains, w_qkv, q_norm, k_norm, cos, sin)


def _attn_kernel(q_ref, k_ref, v_ref, o_ref, *, tk):
    tq = q_ref.shape[1]
    s = k_ref.shape[1]
    q = jnp.concatenate([q_ref[0, :, hd * HEAD_DIM:(hd + 1) * HEAD_DIM] for hd in range(Q_PER_KV)], axis=0)
    rows = Q_PER_KV * tq

    def body(j, carry):
        m, l, acc = carry
        start = pl.multiple_of(j * tk, tk)
        kc = k_ref[0, pl.ds(start, tk), :]
        vc = v_ref[0, pl.ds(start, tk), :]
        sc = lax.dot_general(q, kc, (((1,), (1,)), ((), ())), preferred_element_type=F32)
        m_new = jnp.maximum(m, jnp.max(sc, axis=-1, keepdims=True))
        alpha = jnp.exp(m - m_new)
        p = jnp.exp(sc - m_new)
        l = alpha * l + jnp.sum(p, axis=-1, keepdims=True)
        acc = alpha * acc + _dot(p.astype(BF16), vc)
        return m_new, l, acc

    init = (jnp.full((rows, 1), -jnp.inf, F32), jnp.zeros((rows, 1), F32), jnp.zeros((rows, HEAD_DIM), F32))
    _, l, acc = lax.fori_loop(0, s // tk, body, init)
    o = acc / l
    for hd in range(Q_PER_KV):
        o_ref[0, :, hd * HEAD_DIM:(hd + 1) * HEAD_DIM] = o[hd * tq:(hd + 1) * tq, :].astype(BF16)


def _attn_call(q, k, v, *, tq=128, tk=512):
    b, s, d = q.shape
    gw = Q_PER_KV * HEAD_DIM
    q_spec = pl.BlockSpec((1, tq, gw), lambda bi, g, i: (bi, i, g))
    kv_spec = pl.BlockSpec((1, s, HEAD_DIM), lambda bi, g, i: (bi, 0, g))
    return pl.pallas_call(
        functools.partial(_attn_kernel, tk=tk),
        grid=(b, N_KV_HEADS, s // tq),
        in_specs=[q_spec, kv_spec, kv_spec],
        out_specs=q_spec,
        out_shape=jax.ShapeDtypeStruct(q.shape, BF16),
        compiler_params=_params(3),
        name="attn",
    )(q, k, v)


def _oproj_kernel(x_ref, a_ref, g_ref, w_ref, o_ref):
    y = _dot(a_ref[...], w_ref[...])
    o_ref[...] = x_ref[...] + _rms(y, g_ref[1:2, :])


def _oproj_call(x, a, gains, w, *, tm=512):
    t, d = x.shape
    row_spec = pl.BlockSpec((tm, d), lambda i: (i, 0))
    return pl.pallas_call(
        _oproj_kernel,
        grid=(t // tm,),
        in_specs=[row_spec, row_spec, _const_spec(gains.shape), _const_spec(w.shape)],
        out_specs=row_spec,
        out_shape=jax.ShapeDtypeStruct(x.shape, x.dtype),
        compiler_params=_params(1),
        name="oproj",
    )(x, a, gains, w)


def _dft_cos_sin(n):
    idx = jnp.arange(n, dtype=jnp.int32)
    ang = ((idx[:, None] * idx[None, :]) % n).astype(F32) * (2.0 * math.pi / n)
    return jnp.cos(ang), jnp.sin(ang)


def _fnet_tables(s):
    n1 = s // DFT_N2
    eye = jnp.eye(SUBLANES, dtype=F32)
    c_ch, s_ch = _dft_cos_sin(C_GROUP_DIM)
    chan = (jnp.concatenate([c_ch, s_ch], axis=1) * C_GROUP_DIM ** -0.5).astype(BF16)
    c1, s1 = _dft_cos_sin(n1)
    kc = jnp.kron(c1, eye)
    ks = jnp.kron(s1, eye)
    m_a = (jnp.block([[kc, -ks], [-ks, -kc]]) * s ** -0.5).astype(BF16)
    c2, s2 = _dft_cos_sin(DFT_N2)
    expand = lambda f: jnp.einsum("kn,ab->kabn", f, eye).reshape(DFT_N2 * SUBLANES, SUBLANES * DFT_N2)
    m_b = jnp.concatenate([expand(c2), expand(s2)], axis=1).astype(BF16)
    k1 = jnp.arange(n1, dtype=jnp.int32)
    n2 = jnp.arange(DFT_N2, dtype=jnp.int32)
    ang = ((k1[:, None] * n2[None, :]) % s).astype(F32) * (2.0 * math.pi / s)
    lanes = lambda t: jnp.broadcast_to(t.reshape(s, 1), (s, 128))
    return chan, m_a, m_b, lanes(jnp.cos(ang)), lanes(jnp.sin(ang))


def _fnet_a_kernel(x_ref, g_ref, chan_ref, m_ref, tc_ref, ts_ref, ar_ref, ai_ref):
    n1 = x_ref.shape[0]
    rows = n1 * SUBLANES
    x = x_ref[...].reshape(rows, D_MODEL)
    h = _rms(x, g_ref[0:1, :]).astype(BF16)
    zc, zs = [], []
    for g in range(C_GROUPS):
        z = _dot(h[:, g * C_GROUP_DIM:(g + 1) * C_GROUP_DIM], chan_ref[...])
        zc.append(z[:, :C_GROUP_DIM])
        zs.append(z[:, C_GROUP_DIM:])
    z = jnp.concatenate([jnp.concatenate(zc, axis=1), jnp.concatenate(zs, axis=1)], axis=0).astype(BF16)
    a = _dot(m_ref[...], z)
    a_re, a_im = a[:rows, :], a[rows:, :]
    tc = jnp.concatenate([tc_ref[...].reshape(rows, 128)] * (D_MODEL // 128), axis=1)
    ts = jnp.concatenate([ts_ref[...].reshape(rows, 128)] * (D_MODEL // 128), axis=1)
    ar_ref[...] = (a_re * tc + a_im * ts).reshape(ar_ref.shape)
    ai_ref[...] = (a_im * tc - a_re * ts).reshape(ai_ref.shape)


def _fnet_b_kernel(x_ref, ar_ref, ai_ref, g_ref, m_ref, w_ref, o_ref):
    rows = DFT_N2 * SUBLANES
    a = jnp.concatenate([ar_ref[...], ai_ref[...]], axis=0).astype(BF16)
    f = _dot(m_ref[...], a).astype(BF16)
    y = _dot(f, w_ref[...])
    x = x_ref[...].reshape(rows, D_MODEL)
    o_ref[...] = (x + _rms(y, g_ref[1:2, :])).reshape(o_ref.shape)


def _fnet_call(x, gains, w_out, tables):
    b, s, d = x.shape
    chan, m_a, m_b, tw_c, tw_s = tables
    n1 = s // DFT_N2
    n2a = DFT_N2 // SUBLANES
    k1a = n1 // SUBLANES
    xa = x.reshape(b, n1, n2a, SUBLANES, d)
    a_spec = pl.BlockSpec((None, n1, None, SUBLANES, d), lambda bi, j: (bi, 0, j, 0, 0))
    tw_spec = pl.BlockSpec((n1, None, SUBLANES, 128), lambda bi, j: (0, j, 0, 0))
    tw_shape = (n1, n2a, SUBLANES, 128)
    a_re, a_im = pl.pallas_call(
        _fnet_a_kernel,
        grid=(b, n2a),
        in_specs=[a_spec, _const_spec(gains.shape), _const_spec(chan.shape), _const_spec(m_a.shape),
                  tw_spec, tw_spec],
        out_specs=[a_spec, a_spec],
        out_shape=[jax.ShapeDtypeStruct(xa.shape, F32)] * 2,
        compiler_params=_params(2),
        name="fnet_a",
    )(xa, gains, chan, m_a, tw_c.reshape(tw_shape), tw_s.reshape(tw_shape))
    rows = SUBLANES * DFT_N2
    a_re = a_re.reshape(b, k1a, rows, d)
    a_im = a_im.reshape(b, k1a, rows, d)
    xb = x.reshape(b, DFT_N2, k1a, SUBLANES, d)
    in_spec = pl.BlockSpec((None, None, rows, d), lambda bi, j: (bi, j, 0, 0))
    x_spec = pl.BlockSpec((None, DFT_N2, None, SUBLANES, d), lambda bi, j: (bi, 0, j, 0, 0))
    out = pl.pallas_call(
        _fnet_b_kernel,
        grid=(b, k1a),
        in_specs=[x_spec, in_spec, in_spec, _const_spec(gains.shape), _const_spec(m_b.shape),
                  _const_spec(w_out.shape)],
        out_specs=x_spec,
        out_shape=jax.ShapeDtypeStruct(xb.shape, F32),
        compiler_params=_params(2),
        name="fnet_b",
    )(xb, a_re, a_im, gains, m_b, w_out)
    return out.reshape(b, s, d)


def _trunk(x, p):
    b, s, d = x.shape
    t = b * s
    for i in range(DEPTH):
        m, j = i % N_MIXERS, i // N_MIXERS
        g = p["norm_gains"][i]
        if m == 0:
            x = _gmlp_call(x.reshape(t, d), g, p["a_w_in"][j], p["a_v_norm"][j], p["a_w_s"][j],
                           p["a_b_s_t"][j], p["a_w_out"][j]).reshape(b, s, d)
        elif m == 1:
            cos, sin = _rope_tables(s)
            q, k, v = _qkv_call(x.reshape(t, d), g, p["b_w_qkv"][j], p["b_q_norm"][j], p["b_k_norm"][j], cos, sin)
            kv_w = N_KV_HEADS * HEAD_DIM
            a = _attn_call(q.reshape(b, s, d), k.reshape(b, s, kv_w), v.reshape(b, s, kv_w))
            x = _oproj_call(x.reshape(t, d), a.reshape(t, d), g, p["b_w_o"][j]).reshape(b, s, d)
        else:
            x = _fnet_call(x, g, p["c_w_out"][j], _fnet_tables(s))
        x = _mlp_call(x.reshape(t, d), g, p["mlp_w_in"][i], p["mlp_w_out"][i]).reshape(b, s, d)
    return x


def kernel(x_prompt, x_sample, norm_gains, a_w_in, a_v_norm, a_w_s, a_b_s, a_w_out,
           b_w_qkv, b_q_norm, b_k_norm, b_w_o, c_w_out, mlp_w_in, mlp_w_out):
    p = {
        "norm_gains": norm_gains,
        "a_w_in": a_w_in.astype(BF16),
        "a_v_norm": a_v_norm[:, None, :],
        "a_w_s": a_w_s.astype(BF16),
        "a_b_s_t": jnp.swapaxes(a_b_s, 1, 2),
        "a_w_out": a_w_out.astype(BF16),
        "b_w_qkv": b_w_qkv.astype(BF16),
        "b_q_norm": b_q_norm[:, None, :],
        "b_k_norm": b_k_norm[:, None, :],
        "b_w_o": b_w_o.astype(BF16),
        "c_w_out": c_w_out.astype(BF16),
        "mlp_w_in": mlp_w_in.astype(BF16),
        "mlp_w_out": mlp_w_out.astype(BF16),
    }
    return (_trunk(x_prompt, p), _trunk(x_sample, p))
```

```python
import functools
import math

import jax
import jax.numpy as jnp
from jax import lax
from jax.experimental import pallas as pl
from jax.experimental.pallas import tpu as pltpu

D_MODEL = 1024
DEPTH = 4
N_MIXERS = 3
GRID_W = 64
NORM_EPS = 1e-6
CHUNK = 128
A_WIDTH = 3 * D_MODEL
A_GROUPS = 8
A_GROUP_DIM = A_WIDTH // A_GROUPS
HEAD_DIM = 128
N_HEADS = D_MODEL // HEAD_DIM
N_KV_HEADS = 2
Q_PER_KV = N_HEADS // N_KV_HEADS
AXIS_DIM = HEAD_DIM // 2
ROPE_THETA = 10000.0
C_GROUPS = 8
C_GROUP_DIM = D_MODEL // C_GROUPS
D_FF = 4 * D_MODEL

SUBLANES = 8
VMEM_LIMIT_BYTES = 56 * 1024 * 1024

DFT_N2 = 128

BF16 = jnp.bfloat16
F32 = jnp.float32


def _rms(x, g):
    return x * lax.rsqrt(jnp.mean(x * x, axis=-1, keepdims=True) + NORM_EPS) * g


def _gelu(x):
    return 0.5 * x * (1.0 + lax.erf(x * math.sqrt(0.5)))


def _dot(a, b):
    return jnp.dot(a, b, preferred_element_type=F32)


def _const_spec(shape):
    nd = len(shape)
    return pl.BlockSpec(shape, lambda *_: (0,) * nd, pipeline_mode=pl.Buffered(1))


def _params(n_axes):
    return pltpu.CompilerParams(dimension_semantics=("parallel",) * n_axes,
                                vmem_limit_bytes=VMEM_LIMIT_BYTES)


def _mlp_kernel(x_ref, g_ref, w1_ref, w2_ref, o_ref, *, ff_chunk):
    x = x_ref[...]
    h = _rms(x, g_ref[2:3, :]).astype(BF16)
    acc = jnp.zeros(x.shape, F32)
    for c in range(D_FF // ff_chunk):
        cols = slice(c * ff_chunk, (c + 1) * ff_chunk)
        a = _dot(h, w1_ref[:, cols])
        a = jnp.square(jnp.maximum(a, 0.0)).astype(BF16)
        acc = acc + _dot(a, w2_ref[cols, :])
    o_ref[...] = x + _rms(acc, g_ref[3:4, :])


def _mlp_call(x, gains, w1, w2, *, tm=512, ff_chunk=1024):
    t, d = x.shape
    row_spec = pl.BlockSpec((tm, d), lambda i: (i, 0))
    return pl.pallas_call(
        functools.partial(_mlp_kernel, ff_chunk=ff_chunk),
        grid=(t // tm,),
        in_specs=[row_spec, _const_spec(gains.shape), _const_spec(w1.shape), _const_spec(w2.shape)],
        out_specs=row_spec,
        out_shape=jax.ShapeDtypeStruct(x.shape, x.dtype),
        compiler_params=_params(1),
        name="mlp",
    )(x, gains, w1, w2)


def _gmlp_kernel(x_ref, g_ref, w_in_ref, vn_ref, ws_ref, bs_ref, w_out_ref, o_ref, v_scr):
    x = x_ref[...]
    tm = x.shape[0]
    h = _rms(x, g_ref[0:1, :]).astype(BF16)
    ss = jnp.zeros((tm, 1), F32)
    for g in range(A_GROUPS):
        cols = slice(g * A_GROUP_DIM, (g + 1) * A_GROUP_DIM)
        vg = _gelu(_dot(h, w_in_ref[:, A_WIDTH + g * A_GROUP_DIM:A_WIDTH + (g + 1) * A_GROUP_DIM]))
        ss = ss + jnp.sum(vg * vg, axis=-1, keepdims=True)
        v_scr[:, cols] = vg
    r = lax.rsqrt(ss * (1.0 / A_WIDTH) + NORM_EPS)
    acc = jnp.zeros(x.shape, F32)
    for g in range(A_GROUPS):
        cols = slice(g * A_GROUP_DIM, (g + 1) * A_GROUP_DIM)
        vn = (v_scr[:, cols] * r * vn_ref[:, cols]).astype(BF16)
        u = _gelu(_dot(h, w_in_ref[:, cols]))
        ws = ws_ref[g]
        bias = bs_ref[:, g:g + 1]
        gated = []
        for c in range(tm // CHUNK):
            rows = slice(c * CHUNK, (c + 1) * CHUNK)
            sv = _dot(ws, vn[rows, :]) + bias
            gated.append((u[rows, :] * sv).astype(BF16))
        acc = acc + _dot(jnp.concatenate(gated, axis=0), w_out_ref[cols, :])
    o_ref[...] = x + _rms(acc, g_ref[1:2, :])


def _gmlp_call(x, gains, w_in, v_norm, w_s, b_s_t, w_out, *, tm=512):
    t, d = x.shape
    row_spec = pl.BlockSpec((tm, d), lambda i: (i, 0))
    return pl.pallas_call(
        _gmlp_kernel,
        grid=(t // tm,),
        in_specs=[row_spec, _const_spec(gains.shape), _const_spec(w_in.shape), _const_spec(v_norm.shape),
                  _const_spec(w_s.shape), _const_spec(b_s_t.shape), _const_spec(w_out.shape)],
        out_specs=row_spec,
        out_shape=jax.ShapeDtypeStruct(x.shape, x.dtype),
        scratch_shapes=[pltpu.VMEM((tm, A_WIDTH), F32)],
        compiler_params=_params(1),
        name="gmlp",
    )(x, gains, w_in, v_norm, w_s, b_s_t, w_out)


def _rope_tables(s):
    pos = jnp.arange(s, dtype=jnp.int32)
    row_pos = (pos // GRID_W).astype(F32)
    col_pos = (pos % GRID_W).astype(F32)
    inv_freq = ROPE_THETA ** (-jnp.arange(0, AXIS_DIM, 2, dtype=F32) / AXIS_DIM)
    ang_row = row_pos[:, None] * inv_freq[None, :]
    ang_col = col_pos[:, None] * inv_freq[None, :]
    cos = jnp.concatenate([jnp.cos(ang_row)] * 2 + [jnp.cos(ang_col)] * 2, axis=-1)
    sin = jnp.concatenate([-jnp.sin(ang_row), jnp.sin(ang_row), -jnp.sin(ang_col), jnp.sin(ang_col)], axis=-1)
    return cos, sin


def _qkv_kernel(x_ref, g_ref, w_ref, qn_ref, kn_ref, cos_ref, sin_ref, q_ref, k_ref, v_ref):
    h = _rms(x_ref[...], g_ref[0:1, :]).astype(BF16)
    qkv = _dot(h, w_ref[...])
    cos = cos_ref[...]
    sin = sin_ref[...]
    half = AXIS_DIM // 2
    lane = lax.broadcasted_iota(jnp.int32, cos.shape, 1)
    first_half = (lane % AXIS_DIM) < half
    scale = HEAD_DIM ** -0.5 * math.log2(math.e)

    def norm_rope(t, gain):
        t = _rms(t, gain)
        partner = jnp.where(first_half, pltpu.roll(t, HEAD_DIM - half, 1), pltpu.roll(t, half, 1))
        return t * cos + partner * sin

    for hd in range(N_HEADS):
        cols = slice(hd * HEAD_DIM, (hd + 1) * HEAD_DIM)
        q_ref[:, cols] = (norm_rope(qkv[:, cols], qn_ref[...]) * scale).astype(BF16)
    for hd in range(N_KV_HEADS):
        cols = slice(hd * HEAD_DIM, (hd + 1) * HEAD_DIM)
        kcols = slice((N_HEADS + hd) * HEAD_DIM, (N_HEADS + hd + 1) * HEAD_DIM)
        vcols = slice((N_HEADS + N_KV_HEADS + hd) * HEAD_DIM, (N_HEADS + N_KV_HEADS + hd + 1) * HEAD_DIM)
        k_ref[:, cols] = norm_rope(qkv[:, kcols], kn_ref[...]).astype(BF16)
        v_ref[:, cols] = qkv[:, vcols].astype(BF16)


def _qkv_call(x, gains, w_qkv, q_norm, k_norm, cos, sin, *, tm=512):
    t, d = x.shape
    s = cos.shape[0]
    blocks_per_seq = s // tm
    row = lambda i: (i, 0)
    pos = lambda i: (i % blocks_per_seq, 0)
    kv_w = N_KV_HEADS * HEAD_DIM
    return pl.pallas_call(
        _qkv_kernel,
        grid=(t // tm,),
        in_specs=[pl.BlockSpec((tm, d), row), _const_spec(gains.shape), _const_spec(w_qkv.shape),
                  _const_spec(q_norm.shape), _const_spec(k_norm.shape),
                  pl.BlockSpec((tm, HEAD_DIM), pos), pl.BlockSpec((tm, HEAD_DIM), pos)],
        out_specs=[pl.BlockSpec((tm, d), row), pl.BlockSpec((tm, kv_w), row), pl.BlockSpec((tm, kv_w), row)],
        out_shape=[jax.ShapeDtypeStruct((t, d), BF16), jax.ShapeDtypeStruct((t, kv_w), BF16),
                   jax.ShapeDtypeStruct((t, kv_w), BF16)],
        compiler_params=_params(1),
        name="qkv",
    )(x, gains, w_qkv, q_norm, k_norm, cos, sin)


def _attn_kernel(q_ref, k_ref, v_ref, o_ref, *, tk, unroll):
    tq = q_ref.shape[1]
    s = k_ref.shape[1]
    q = jnp.concatenate([q_ref[0, :, hd * HEAD_DIM:(hd + 1) * HEAD_DIM] for hd in range(Q_PER_KV)], axis=0)
    rows = Q_PER_KV * tq
    ones = jnp.ones((tk, HEAD_DIM), BF16)

    def body(j, carry):
        m, acc = carry
        start = pl.multiple_of(j * tk, tk)
        kc = k_ref[0, pl.ds(start, tk), :]
        vc = jnp.concatenate([v_ref[0, pl.ds(start, tk), :], ones], axis=1)
        sc = lax.dot_general(q, kc, (((1,), (1,)), ((), ())), preferred_element_type=F32)
        m_new = jnp.maximum(m, jnp.max(sc, axis=-1, keepdims=True))
        alpha = jnp.exp2(m - m_new)
        p = jnp.exp2(sc - m_new)
        acc = alpha * acc + _dot(p.astype(BF16), vc)
        return m_new, acc

    init = (jnp.full((rows, 1), -jnp.inf, F32), jnp.zeros((rows, 2 * HEAD_DIM), F32))
    _, acc = lax.fori_loop(0, s // tk, body, init, unroll=unroll)
    o = acc[:, :HEAD_DIM] / acc[:, HEAD_DIM:]
    for hd in range(Q_PER_KV):
        o_ref[0, :, hd * HEAD_DIM:(hd + 1) * HEAD_DIM] = o[hd * tq:(hd + 1) * tq, :].astype(BF16)


def _attn_call(q, k, v, *, tq=256, tk=512, unroll=4):
    b, s, d = q.shape
    gw = Q_PER_KV * HEAD_DIM
    q_spec = pl.BlockSpec((1, tq, gw), lambda bi, g, i: (bi, i, g))
    kv_spec = pl.BlockSpec((1, s, HEAD_DIM), lambda bi, g, i: (bi, 0, g))
    return pl.pallas_call(
        functools.partial(_attn_kernel, tk=tk, unroll=unroll),
        grid=(b, N_KV_HEADS, s // tq),
        in_specs=[q_spec, kv_spec, kv_spec],
        out_specs=q_spec,
        out_shape=jax.ShapeDtypeStruct(q.shape, BF16),
        compiler_params=_params(3),
        name="attn",
    )(q, k, v)


def _oproj_kernel(x_ref, a_ref, g_ref, w_ref, o_ref):
    y = _dot(a_ref[...], w_ref[...])
    o_ref[...] = x_ref[...] + _rms(y, g_ref[1:2, :])


def _oproj_call(x, a, gains, w, *, tm=512):
    t, d = x.shape
    row_spec = pl.BlockSpec((tm, d), lambda i: (i, 0))
    return pl.pallas_call(
        _oproj_kernel,
        grid=(t // tm,),
        in_specs=[row_spec, row_spec, _const_spec(gains.shape), _const_spec(w.shape)],
        out_specs=row_spec,
        out_shape=jax.ShapeDtypeStruct(x.shape, x.dtype),
        compiler_params=_params(1),
        name="oproj",
    )(x, a, gains, w)


def _dft_cos_sin(n):
    idx = jnp.arange(n, dtype=jnp.int32)
    ang = ((idx[:, None] * idx[None, :]) % n).astype(F32) * (2.0 * math.pi / n)
    return jnp.cos(ang), jnp.sin(ang)


def _fnet_tables(s):
    n1 = s // DFT_N2
    eye = jnp.eye(SUBLANES, dtype=F32)
    c_ch, s_ch = _dft_cos_sin(C_GROUP_DIM)
    chan = (jnp.concatenate([c_ch, s_ch], axis=1) * C_GROUP_DIM ** -0.5).astype(BF16)
    c1, s1 = _dft_cos_sin(n1)
    kc = jnp.kron(c1, eye)
    ks = jnp.kron(s1, eye)
    m_a = (jnp.block([[kc, -ks], [-ks, -kc]]) * s ** -0.5).astype(BF16)
    c2, s2 = _dft_cos_sin(DFT_N2)
    expand = lambda f: jnp.einsum("kn,ab->kabn", f, eye).reshape(DFT_N2 * SUBLANES, SUBLANES * DFT_N2)
    m_b = jnp.concatenate([expand(c2), expand(s2)], axis=1).astype(BF16)
    k1 = jnp.arange(n1, dtype=jnp.int32)
    n2 = jnp.arange(DFT_N2, dtype=jnp.int32)
    ang = ((k1[:, None] * n2[None, :]) % s).astype(F32) * (2.0 * math.pi / s)
    lanes = lambda t: jnp.broadcast_to(t.reshape(s, 1), (s, 128))
    return chan, m_a, m_b, lanes(jnp.cos(ang)), lanes(jnp.sin(ang))


def _fnet_a_kernel(x_ref, g_ref, chan_ref, m_ref, tc_ref, ts_ref, ar_ref, ai_ref):
    n1 = x_ref.shape[0]
    rows = n1 * SUBLANES
    x = x_ref[...].reshape(rows, D_MODEL)
    h = _rms(x, g_ref[0:1, :]).astype(BF16)
    zc, zs = [], []
    for g in range(C_GROUPS):
        z = _dot(h[:, g * C_GROUP_DIM:(g + 1) * C_GROUP_DIM], chan_ref[...])
        zc.append(z[:, :C_GROUP_DIM])
        zs.append(z[:, C_GROUP_DIM:])
    z = jnp.concatenate([jnp.concatenate(zc, axis=1), jnp.concatenate(zs, axis=1)], axis=0).astype(BF16)
    a = _dot(m_ref[...], z)
    a_re, a_im = a[:rows, :], a[rows:, :]
    tc = jnp.concatenate([tc_ref[...].reshape(rows, 128)] * (D_MODEL // 128), axis=1)
    ts = jnp.concatenate([ts_ref[...].reshape(rows, 128)] * (D_MODEL // 128), axis=1)
    ar_ref[...] = (a_re * tc + a_im * ts).reshape(ar_ref.shape)
    ai_ref[...] = (a_im * tc - a_re * ts).reshape(ai_ref.shape)


def _fnet_b_kernel(x_ref, ar_ref, ai_ref, g_ref, m_ref, w_ref, o_ref):
    rows = DFT_N2 * SUBLANES
    a = jnp.concatenate([ar_ref[...], ai_ref[...]], axis=0).astype(BF16)
    f = _dot(m_ref[...], a).astype(BF16)
    y = _dot(f, w_ref[...])
    x = x_ref[...].reshape(rows, D_MODEL)
    o_ref[...] = (x + _rms(y, g_ref[1:2, :])).reshape(o_ref.shape)


def _fnet_call(x, gains, w_out, tables):
    b, s, d = x.shape
    chan, m_a, m_b, tw_c, tw_s = tables
    n1 = s // DFT_N2
    n2a = DFT_N2 // SUBLANES
    k1a = n1 // SUBLANES
    xa = x.reshape(b, n1, n2a, SUBLANES, d)
    a_spec = pl.BlockSpec((None, n1, None, SUBLANES, d), lambda bi, j: (bi, 0, j, 0, 0))
    tw_spec = pl.BlockSpec((n1, None, SUBLANES, 128), lambda bi, j: (0, j, 0, 0))
    tw_shape = (n1, n2a, SUBLANES, 128)
    a_re, a_im = pl.pallas_call(
        _fnet_a_kernel,
        grid=(b, n2a),
        in_specs=[a_spec, _const_spec(gains.shape), _const_spec(chan.shape), _const_spec(m_a.shape),
                  tw_spec, tw_spec],
        out_specs=[a_spec, a_spec],
        out_shape=[jax.ShapeDtypeStruct(xa.shape, F32)] * 2,
        compiler_params=_params(2),
        name="fnet_a",
    )(xa, gains, chan, m_a, tw_c.reshape(tw_shape), tw_s.reshape(tw_shape))
    rows = SUBLANES * DFT_N2
    a_re = a_re.reshape(b, k1a, rows, d)
    a_im = a_im.reshape(b, k1a, rows, d)
    xb = x.reshape(b, DFT_N2, k1a, SUBLANES, d)
    in_spec = pl.BlockSpec((None, None, rows, d), lambda bi, j: (bi, j, 0, 0))
    x_spec = pl.BlockSpec((None, DFT_N2, None, SUBLANES, d), lambda bi, j: (bi, 0, j, 0, 0))
    out = pl.pallas_call(
        _fnet_b_kernel,
        grid=(b, k1a),
        in_specs=[x_spec, in_spec, in_spec, _const_spec(gains.shape), _const_spec(m_b.shape),
                  _const_spec(w_out.shape)],
        out_specs=x_spec,
        out_shape=jax.ShapeDtypeStruct(xb.shape, F32),
        compiler_params=_params(2),
        name="fnet_b",
    )(xb, a_re, a_im, gains, m_b, w_out)
    return out.reshape(b, s, d)


def _trunk(x, p):
    b, s, d = x.shape
    t = b * s
    for i in range(DEPTH):
        m, j = i % N_MIXERS, i // N_MIXERS
        g = p["norm_gains"][i]
        if m == 0:
            x = _gmlp_call(x.reshape(t, d), g, p["a_w_in"][j], p["a_v_norm"][j], p["a_w_s"][j],
                           p["a_b_s_t"][j], p["a_w_out"][j]).reshape(b, s, d)
        elif m == 1:
            cos, sin = _rope_tables(s)
            q, k, v = _qkv_call(x.reshape(t, d), g, p["b_w_qkv"][j], p["b_q_norm"][j], p["b_k_norm"][j], cos, sin)
            kv_w = N_KV_HEADS * HEAD_DIM
            a = _attn_call(q.reshape(b, s, d), k.reshape(b, s, kv_w), v.reshape(b, s, kv_w))
            x = _oproj_call(x.reshape(t, d), a.reshape(t, d), g, p["b_w_o"][j]).reshape(b, s, d)
        else:
            x = _fnet_call(x, g, p["c_w_out"][j], _fnet_tables(s))
        x = _mlp_call(x.reshape(t, d), g, p["mlp_w_in"][i], p["mlp_w_out"][i]).reshape(b, s, d)
    return x


def kernel(x_prompt, x_sample, norm_gains, a_w_in, a_v_norm, a_w_s, a_b_s, a_w_out,
           b_w_qkv, b_q_norm, b_k_norm, b_w_o, c_w_out, mlp_w_in, mlp_w_out):
    p = {
        "norm_gains": norm_gains,
        "a_w_in": a_w_in.astype(BF16),
        "a_v_norm": a_v_norm[:, None, :],
        "a_w_s": a_w_s.astype(BF16),
        "a_b_s_t": jnp.swapaxes(a_b_s, 1, 2),
        "a_w_out": a_w_out.astype(BF16),
        "b_w_qkv": b_w_qkv.astype(BF16),
        "b_q_norm": b_q_norm[:, None, :],
        "b_k_norm": b_k_norm[:, None, :],
        "b_w_o": b_w_o.astype(BF16),
        "c_w_out": c_w_out.astype(BF16),
        "mlp_w_in": mlp_w_in.astype(BF16),
        "mlp_w_out": mlp_w_out.astype(BF16),
    }
    return (_trunk(x_prompt, p), _trunk(x_sample, p))
```

```python
import functools
import math
from typing import NamedTuple

import jax
import jax.numpy as jnp
from jax import lax
from jax.experimental import pallas as pl
from jax.experimental.pallas import tpu as pltpu

D_MODEL = 1024
DEPTH = 4
N_MIXERS = 3
GRID_W = 64
NORM_EPS = 1e-6
CHUNK = 128
A_WIDTH = 3 * D_MODEL
A_GROUPS = 8
A_GROUP_DIM = A_WIDTH // A_GROUPS
HEAD_DIM = 128
N_HEADS = D_MODEL // HEAD_DIM
N_KV_HEADS = 2
Q_PER_KV = N_HEADS // N_KV_HEADS
AXIS_DIM = HEAD_DIM // 2
ROPE_THETA = 10000.0
C_GROUPS = 8
C_GROUP_DIM = D_MODEL // C_GROUPS
D_FF = 4 * D_MODEL

SUBLANES = 8
VMEM_LIMIT_BYTES = 56 * 1024 * 1024

DFT_N2 = 128

BF16 = jnp.bfloat16
F32 = jnp.float32


def _rms(x, g):
    return x * lax.rsqrt(jnp.mean(x * x, axis=-1, keepdims=True) + NORM_EPS) * g


def _gelu(x):
    return 0.5 * x * (1.0 + lax.erf(x * math.sqrt(0.5)))


def _dot(a, b):
    return jnp.dot(a, b, preferred_element_type=F32)


class _Layer(NamedTuple):
    stacked: jax.Array
    index: int


def _const_spec(operand):
    if isinstance(operand, _Layer):
        shape = (None,) + operand.stacked.shape[1:]
        index = (operand.index,) + (0,) * (len(shape) - 1)
    else:
        shape, index = operand.shape, (0,) * operand.ndim
    return pl.BlockSpec(shape, lambda *_: index, pipeline_mode=pl.Buffered(1))


def _raw(operand):
    return operand.stacked if isinstance(operand, _Layer) else operand


def _params(n_axes):
    return pltpu.CompilerParams(dimension_semantics=("parallel",) * n_axes,
                                vmem_limit_bytes=VMEM_LIMIT_BYTES)


def _mlp_kernel(x_ref, g_ref, w1_ref, w2_ref, o_ref, *, ff_chunk):
    x = x_ref[...]
    h = _rms(x, g_ref[2:3, :]).astype(BF16)
    acc = jnp.zeros(x.shape, F32)
    for c in range(D_FF // ff_chunk):
        cols = slice(c * ff_chunk, (c + 1) * ff_chunk)
        a = _dot(h, w1_ref[:, cols])
        a = jnp.square(jnp.maximum(a, 0.0)).astype(BF16)
        acc = acc + _dot(a, w2_ref[cols, :])
    o_ref[...] = x + _rms(acc, g_ref[3:4, :])


def _mlp_call(x, gains, w1, w2, *, tm=512, ff_chunk=1024):
    t, d = x.shape
    row_spec = pl.BlockSpec((tm, d), lambda i: (i, 0))
    return pl.pallas_call(
        functools.partial(_mlp_kernel, ff_chunk=ff_chunk),
        grid=(t // tm,),
        in_specs=[row_spec, _const_spec(gains), _const_spec(w1), _const_spec(w2)],
        out_specs=row_spec,
        out_shape=jax.ShapeDtypeStruct(x.shape, x.dtype),
        compiler_params=_params(1),
        name="mlp",
    )(x, _raw(gains), _raw(w1), _raw(w2))


def _gmlp_kernel(x_ref, g_ref, w_in_ref, vn_ref, ws_ref, bs_ref, w_out_ref, o_ref, v_scr):
    x = x_ref[...]
    tm = x.shape[0]
    h = _rms(x, g_ref[0:1, :]).astype(BF16)
    pair = 2 * A_GROUP_DIM
    ss = jnp.zeros((tm, 1), F32)
    for gp in range(A_GROUPS // 2):
        cols = slice(gp * pair, (gp + 1) * pair)
        vg = _gelu(_dot(h, w_in_ref[:, A_WIDTH + gp * pair:A_WIDTH + (gp + 1) * pair]))
        ss = ss + jnp.sum(vg * vg, axis=-1, keepdims=True)
        v_scr[:, cols] = vg
    r = lax.rsqrt(ss * (1.0 / A_WIDTH) + NORM_EPS)
    acc = jnp.zeros(x.shape, F32)
    for gp in range(A_GROUPS // 2):
        cols = slice(gp * pair, (gp + 1) * pair)
        vn = (v_scr[:, cols] * r * vn_ref[:, cols]).astype(BF16)
        u = _gelu(_dot(h, w_in_ref[:, cols]))
        gated = []
        for c in range(tm // CHUNK):
            rows = slice(c * CHUNK, (c + 1) * CHUNK)
            halves = []
            for k in range(2):
                g = 2 * gp + k
                gcols = slice(k * A_GROUP_DIM, (k + 1) * A_GROUP_DIM)
                sv = _dot(ws_ref[g], vn[rows, gcols]) + bs_ref[:, g:g + 1]
                halves.append((u[rows, gcols] * sv).astype(BF16))
            gated.append(jnp.concatenate(halves, axis=1))
        acc = acc + _dot(jnp.concatenate(gated, axis=0), w_out_ref[cols, :])
    o_ref[...] = x + _rms(acc, g_ref[1:2, :])


def _gmlp_call(x, gains, w_in, v_norm, w_s, b_s_t, w_out, *, tm=512):
    t, d = x.shape
    row_spec = pl.BlockSpec((tm, d), lambda i: (i, 0))
    return pl.pallas_call(
        _gmlp_kernel,
        grid=(t // tm,),
        in_specs=[row_spec, _const_spec(gains), _const_spec(w_in), _const_spec(v_norm),
                  _const_spec(w_s), _const_spec(b_s_t), _const_spec(w_out)],
        out_specs=row_spec,
        out_shape=jax.ShapeDtypeStruct(x.shape, x.dtype),
        scratch_shapes=[pltpu.VMEM((tm, A_WIDTH), F32)],
        compiler_params=_params(1),
        name="gmlp",
    )(x, _raw(gains), _raw(w_in), _raw(v_norm), _raw(w_s), _raw(b_s_t), _raw(w_out))


def _rope_tables(s):
    pos = jnp.arange(s, dtype=jnp.int32)
    row_pos = (pos // GRID_W).astype(F32)
    col_pos = (pos % GRID_W).astype(F32)
    inv_freq = ROPE_THETA ** (-jnp.arange(0, AXIS_DIM, 2, dtype=F32) / AXIS_DIM)
    ang_row = row_pos[:, None] * inv_freq[None, :]
    ang_col = col_pos[:, None] * inv_freq[None, :]
    cos = jnp.concatenate([jnp.cos(ang_row)] * 2 + [jnp.cos(ang_col)] * 2, axis=-1)
    sin = jnp.concatenate([-jnp.sin(ang_row), jnp.sin(ang_row), -jnp.sin(ang_col), jnp.sin(ang_col)], axis=-1)
    return cos, sin


def _qkv_kernel(x_ref, g_ref, w_ref, qn_ref, kn_ref, cos_ref, sin_ref, q_ref, k_ref, v_ref):
    h = _rms(x_ref[...], g_ref[0:1, :]).astype(BF16)
    qkv = _dot(h, w_ref[...])
    cos = cos_ref[...]
    sin = sin_ref[...]
    half = AXIS_DIM // 2
    lane = lax.broadcasted_iota(jnp.int32, cos.shape, 1)
    first_half = (lane % AXIS_DIM) < half
    scale = HEAD_DIM ** -0.5 * math.log2(math.e)

    def norm_rope(t, gain):
        t = _rms(t, gain)
        partner = jnp.where(first_half, pltpu.roll(t, HEAD_DIM - half, 1), pltpu.roll(t, half, 1))
        return t * cos + partner * sin

    for hd in range(N_HEADS):
        cols = slice(hd * HEAD_DIM, (hd + 1) * HEAD_DIM)
        q_ref[:, cols] = (norm_rope(qkv[:, cols], qn_ref[...]) * scale).astype(BF16)
    for hd in range(N_KV_HEADS):
        cols = slice(hd * HEAD_DIM, (hd + 1) * HEAD_DIM)
        kcols = slice((N_HEADS + hd) * HEAD_DIM, (N_HEADS + hd + 1) * HEAD_DIM)
        vcols = slice((N_HEADS + N_KV_HEADS + hd) * HEAD_DIM, (N_HEADS + N_KV_HEADS + hd + 1) * HEAD_DIM)
        k_ref[:, cols] = norm_rope(qkv[:, kcols], kn_ref[...]).astype(BF16)
        v_ref[:, cols] = qkv[:, vcols].astype(BF16)


def _qkv_call(x, gains, w_qkv, q_norm, k_norm, cos, sin, *, tm=512):
    t, d = x.shape
    s = cos.shape[0]
    blocks_per_seq = s // tm
    row = lambda i: (i, 0)
    pos = lambda i: (i % blocks_per_seq, 0)
    kv_w = N_KV_HEADS * HEAD_DIM
    return pl.pallas_call(
        _qkv_kernel,
        grid=(t // tm,),
        in_specs=[pl.BlockSpec((tm, d), row), _const_spec(gains), _const_spec(w_qkv),
                  _const_spec(q_norm), _const_spec(k_norm),
                  pl.BlockSpec((tm, HEAD_DIM), pos), pl.BlockSpec((tm, HEAD_DIM), pos)],
        out_specs=[pl.BlockSpec((tm, d), row), pl.BlockSpec((tm, kv_w), row), pl.BlockSpec((tm, kv_w), row)],
        out_shape=[jax.ShapeDtypeStruct((t, d), BF16), jax.ShapeDtypeStruct((t, kv_w), BF16),
                   jax.ShapeDtypeStruct((t, kv_w), BF16)],
        compiler_params=_params(1),
        name="qkv",
    )(x, _raw(gains), _raw(w_qkv), _raw(q_norm), _raw(k_norm), cos, sin)


def _attn_kernel(q_ref, k_ref, v_ref, o_ref, *, tk, unroll):
    tq = q_ref.shape[1]
    s = k_ref.shape[1]
    q = jnp.concatenate([q_ref[0, :, hd * HEAD_DIM:(hd + 1) * HEAD_DIM] for hd in range(Q_PER_KV)], axis=0)
    rows = Q_PER_KV * tq
    ones = jnp.ones((tk, HEAD_DIM), BF16)

    def body(j, carry):
        m, acc = carry
        start = pl.multiple_of(j * tk, tk)
        kc = k_ref[0, pl.ds(start, tk), :]
        vc = jnp.concatenate([v_ref[0, pl.ds(start, tk), :], ones], axis=1)
        sc = lax.dot_general(q, kc, (((1,), (1,)), ((), ())), preferred_element_type=F32)
        m_new = jnp.maximum(m, jnp.max(sc, axis=-1, keepdims=True))
        alpha = jnp.exp2(m - m_new)
        p = jnp.exp2(sc - m_new)
        acc = alpha * acc + _dot(p.astype(BF16), vc)
        return m_new, acc

    init = (jnp.full((rows, 1), -jnp.inf, F32), jnp.zeros((rows, 2 * HEAD_DIM), F32))
    _, acc = lax.fori_loop(0, s // tk, body, init, unroll=unroll)
    o = acc[:, :HEAD_DIM] / acc[:, HEAD_DIM:]
    for hd in range(Q_PER_KV):
        o_ref[0, :, hd * HEAD_DIM:(hd + 1) * HEAD_DIM] = o[hd * tq:(hd + 1) * tq, :].astype(BF16)


def _attn_call(q, k, v, *, tq=256, tk=512, unroll=4):
    b, s, d = q.shape
    gw = Q_PER_KV * HEAD_DIM
    q_spec = pl.BlockSpec((1, tq, gw), lambda bi, g, i: (bi, i, g))
    kv_spec = pl.BlockSpec((1, s, HEAD_DIM), lambda bi, g, i: (bi, 0, g))
    return pl.pallas_call(
        functools.partial(_attn_kernel, tk=tk, unroll=unroll),
        grid=(b, N_KV_HEADS, s // tq),
        in_specs=[q_spec, kv_spec, kv_spec],
        out_specs=q_spec,
        out_shape=jax.ShapeDtypeStruct(q.shape, BF16),
        compiler_params=_params(3),
        name="attn",
    )(q, k, v)


def _oproj_kernel(x_ref, a_ref, g_ref, w_ref, o_ref):
    y = _dot(a_ref[...], w_ref[...])
    o_ref[...] = x_ref[...] + _rms(y, g_ref[1:2, :])


def _oproj_call(x, a, gains, w, *, tm=512):
    t, d = x.shape
    row_spec = pl.BlockSpec((tm, d), lambda i: (i, 0))
    return pl.pallas_call(
        _oproj_kernel,
        grid=(t // tm,),
        in_specs=[row_spec, row_spec, _const_spec(gains), _const_spec(w)],
        out_specs=row_spec,
        out_shape=jax.ShapeDtypeStruct(x.shape, x.dtype),
        compiler_params=_params(1),
        name="oproj",
    )(x, a, _raw(gains), _raw(w))


def _dft_cos_sin(n):
    idx = jnp.arange(n, dtype=jnp.int32)
    ang = ((idx[:, None] * idx[None, :]) % n).astype(F32) * (2.0 * math.pi / n)
    return jnp.cos(ang), jnp.sin(ang)


def _fnet_tables(s):
    n1 = s // DFT_N2
    eye = jnp.eye(SUBLANES, dtype=F32)
    c_ch, s_ch = _dft_cos_sin(C_GROUP_DIM)
    chan = (jnp.concatenate([c_ch, s_ch], axis=1) * C_GROUP_DIM ** -0.5).astype(BF16)
    c1, s1 = _dft_cos_sin(n1)
    kc = jnp.kron(c1, eye)
    ks = jnp.kron(s1, eye)
    m_a = (jnp.block([[kc, -ks], [-ks, -kc]]) * s ** -0.5).astype(BF16)
    c2, s2 = _dft_cos_sin(DFT_N2)
    expand = lambda f: jnp.einsum("kn,ab->kabn", f, eye).reshape(DFT_N2 * SUBLANES, SUBLANES * DFT_N2)
    m_b = jnp.concatenate([expand(c2), expand(s2)], axis=1).astype(BF16)
    k1 = jnp.arange(n1, dtype=jnp.int32)
    n2 = jnp.arange(DFT_N2, dtype=jnp.int32)
    ang = ((k1[:, None] * n2[None, :]) % s).astype(F32) * (2.0 * math.pi / s)
    lanes = lambda t: jnp.broadcast_to(t.reshape(s, 1), (s, 128))
    return chan, m_a, m_b, lanes(jnp.cos(ang)), lanes(jnp.sin(ang))


def _fnet_a_kernel(x_ref, g_ref, chan_ref, m_ref, tc_ref, ts_ref, ar_ref, ai_ref):
    n1 = x_ref.shape[0]
    rows = n1 * SUBLANES
    x = x_ref[...].reshape(rows, D_MODEL)
    h = _rms(x, g_ref[0:1, :]).astype(BF16)
    zc, zs = [], []
    for g in range(C_GROUPS):
        z = _dot(h[:, g * C_GROUP_DIM:(g + 1) * C_GROUP_DIM], chan_ref[...])
        zc.append(z[:, :C_GROUP_DIM])
        zs.append(z[:, C_GROUP_DIM:])
    z = jnp.concatenate([jnp.concatenate(zc, axis=1), jnp.concatenate(zs, axis=1)], axis=0).astype(BF16)
    a = _dot(m_ref[...], z)
    a_re, a_im = a[:rows, :], a[rows:, :]
    tc = jnp.concatenate([tc_ref[...].reshape(rows, 128)] * (D_MODEL // 128), axis=1)
    ts = jnp.concatenate([ts_ref[...].reshape(rows, 128)] * (D_MODEL // 128), axis=1)
    ar_ref[...] = (a_re * tc + a_im * ts).reshape(ar_ref.shape)
    ai_ref[...] = (a_im * tc - a_re * ts).reshape(ai_ref.shape)


def _fnet_b_kernel(x_ref, ar_ref, ai_ref, g_ref, m_ref, w_ref, o_ref):
    rows = DFT_N2 * SUBLANES
    a = jnp.concatenate([ar_ref[...].reshape(rows, D_MODEL), ai_ref[...].reshape(rows, D_MODEL)],
                        axis=0).astype(BF16)
    f = _dot(m_ref[...], a).astype(BF16)
    y = _dot(f, w_ref[...])
    x = x_ref[...].reshape(rows, D_MODEL)
    o_ref[...] = (x + _rms(y, g_ref[1:2, :])).reshape(o_ref.shape)


def _fnet_call(x, gains, w_out, tables):
    b, s, d = x.shape
    chan, m_a, m_b, tw_c, tw_s = tables
    n1 = s // DFT_N2
    n2a = DFT_N2 // SUBLANES
    k1a = n1 // SUBLANES
    xa = x.reshape(b, n1, n2a, SUBLANES, d)
    a_spec = pl.BlockSpec((None, n1, None, SUBLANES, d), lambda bi, j: (bi, 0, j, 0, 0))
    tw_spec = pl.BlockSpec((n1, None, SUBLANES, 128), lambda bi, j: (0, j, 0, 0))
    tw_shape = (n1, n2a, SUBLANES, 128)
    a_re, a_im = pl.pallas_call(
        _fnet_a_kernel,
        grid=(b, n2a),
        in_specs=[a_spec, _const_spec(gains), _const_spec(chan), _const_spec(m_a), tw_spec, tw_spec],
        out_specs=[a_spec, a_spec],
        out_shape=[jax.ShapeDtypeStruct(xa.shape, F32)] * 2,
        compiler_params=_params(2),
        name="fnet_a",
    )(xa, _raw(gains), chan, m_a, tw_c.reshape(tw_shape), tw_s.reshape(tw_shape))
    xb = x.reshape(b, DFT_N2, k1a, SUBLANES, d)
    in_spec = pl.BlockSpec((None, SUBLANES, n2a, SUBLANES, d), lambda bi, j: (bi, j, 0, 0, 0))
    x_spec = pl.BlockSpec((None, DFT_N2, None, SUBLANES, d), lambda bi, j: (bi, 0, j, 0, 0))
    out = pl.pallas_call(
        _fnet_b_kernel,
        grid=(b, k1a),
        in_specs=[x_spec, in_spec, in_spec, _const_spec(gains), _const_spec(m_b), _const_spec(w_out)],
        out_specs=x_spec,
        out_shape=jax.ShapeDtypeStruct(xb.shape, F32),
        compiler_params=_params(2),
        name="fnet_b",
    )(xb, a_re, a_im, _raw(gains), m_b, _raw(w_out))
    return out.reshape(b, s, d)


def _trunk(x, p):
    b, s, d = x.shape
    t = b * s
    for i in range(DEPTH):
        m, j = i % N_MIXERS, i // N_MIXERS
        g = _Layer(p["norm_gains"], i)
        mixer = lambda name: _Layer(p[name], j)
        if m == 0:
            x = _gmlp_call(x.reshape(t, d), g, mixer("a_w_in"), mixer("a_v_norm"), mixer("a_w_s"),
                           mixer("a_b_s_t"), mixer("a_w_out")).reshape(b, s, d)
        elif m == 1:
            cos, sin = _rope_tables(s)
            q, k, v = _qkv_call(x.reshape(t, d), g, mixer("b_w_qkv"), mixer("b_q_norm"), mixer("b_k_norm"),
                                cos, sin)
            kv_w = N_KV_HEADS * HEAD_DIM
            a = _attn_call(q.reshape(b, s, d), k.reshape(b, s, kv_w), v.reshape(b, s, kv_w))
            x = _oproj_call(x.reshape(t, d), a.reshape(t, d), g, mixer("b_w_o")).reshape(b, s, d)
        else:
            x = _fnet_call(x, g, mixer("c_w_out"), _fnet_tables(s))
        x = _mlp_call(x.reshape(t, d), g, _Layer(p["mlp_w_in"], i), _Layer(p["mlp_w_out"], i)).reshape(b, s, d)
    return x


def kernel(x_prompt, x_sample, norm_gains, a_w_in, a_v_norm, a_w_s, a_b_s, a_w_out,
           b_w_qkv, b_q_norm, b_k_norm, b_w_o, c_w_out, mlp_w_in, mlp_w_out):
    p = {
        "norm_gains": norm_gains,
        "a_w_in": a_w_in.astype(BF16),
        "a_v_norm": a_v_norm[:, None, :],
        "a_w_s": a_w_s.astype(BF16),
        "a_b_s_t": jnp.swapaxes(a_b_s, 1, 2),
        "a_w_out": a_w_out.astype(BF16),
        "b_w_qkv": b_w_qkv.astype(BF16),
        "b_q_norm": b_q_norm[:, None, :],
        "b_k_norm": b_k_norm[:, None, :],
        "b_w_o": b_w_o.astype(BF16),
        "c_w_out": c_w_out.astype(BF16),
        "mlp_w_in": mlp_w_in.astype(BF16),
        "mlp_w_out": mlp_w_out.astype(BF16),
    }
    return (_trunk(x_prompt, p), _trunk(x_sample, p))
```

```python
import functools
import math
from typing import NamedTuple

import jax
import jax.numpy as jnp
from jax import lax
from jax.experimental import pallas as pl
from jax.experimental.pallas import tpu as pltpu

D_MODEL = 1024
DEPTH = 4
N_MIXERS = 3
GRID_W = 64
NORM_EPS = 1e-6
CHUNK = 128
A_WIDTH = 3 * D_MODEL
A_GROUPS = 8
A_GROUP_DIM = A_WIDTH // A_GROUPS
HEAD_DIM = 128
N_HEADS = D_MODEL // HEAD_DIM
N_KV_HEADS = 2
Q_PER_KV = N_HEADS // N_KV_HEADS
AXIS_DIM = HEAD_DIM // 2
ROPE_THETA = 10000.0
C_GROUPS = 8
C_GROUP_DIM = D_MODEL // C_GROUPS
D_FF = 4 * D_MODEL

SUBLANES = 8
VMEM_LIMIT_BYTES = 56 * 1024 * 1024

DFT_N2 = 128

BF16 = jnp.bfloat16
F32 = jnp.float32


def _rms(x, g):
    return x * lax.rsqrt(jnp.mean(x * x, axis=-1, keepdims=True) + NORM_EPS) * g


def _gelu(x):
    return 0.5 * x * (1.0 + lax.erf(x * math.sqrt(0.5)))


def _dot(a, b):
    return jnp.dot(a, b, preferred_element_type=F32)


class _Layer(NamedTuple):
    stacked: jax.Array
    index: int


def _const_spec(operand):
    if isinstance(operand, _Layer):
        shape = (None,) + operand.stacked.shape[1:]
        index = (operand.index,) + (0,) * (len(shape) - 1)
    else:
        shape, index = operand.shape, (0,) * operand.ndim
    return pl.BlockSpec(shape, lambda *_: index, pipeline_mode=pl.Buffered(1))


def _raw(operand):
    return operand.stacked if isinstance(operand, _Layer) else operand


def _params(n_axes):
    return pltpu.CompilerParams(dimension_semantics=("parallel",) * n_axes,
                                vmem_limit_bytes=VMEM_LIMIT_BYTES)


def _mlp_kernel(x_ref, g_ref, w1_ref, w2_ref, o_ref, *, ff_chunk):
    x = x_ref[...]
    h = _rms(x, g_ref[2:3, :]).astype(BF16)
    acc = jnp.zeros(x.shape, F32)
    for c in range(D_FF // ff_chunk):
        cols = slice(c * ff_chunk, (c + 1) * ff_chunk)
        a = _dot(h, w1_ref[:, cols])
        a = jnp.square(jnp.maximum(a, 0.0)).astype(BF16)
        acc = acc + _dot(a, w2_ref[cols, :])
    o_ref[...] = x + _rms(acc, g_ref[3:4, :])


def _mlp_call(x, gains, w1, w2, *, tm=512, ff_chunk=1024):
    t, d = x.shape
    row_spec = pl.BlockSpec((tm, d), lambda i: (i, 0))
    return pl.pallas_call(
        functools.partial(_mlp_kernel, ff_chunk=ff_chunk),
        grid=(t // tm,),
        in_specs=[row_spec, _const_spec(gains), _const_spec(w1), _const_spec(w2)],
        out_specs=row_spec,
        out_shape=jax.ShapeDtypeStruct(x.shape, x.dtype),
        compiler_params=_params(1),
        name="mlp",
    )(x, _raw(gains), _raw(w1), _raw(w2))


def _gmlp_kernel(x_ref, g_ref, w_in_ref, vn_ref, ws_ref, bs_ref, w_out_ref, o_ref, v_scr):
    x = x_ref[...]
    tm = x.shape[0]
    h = _rms(x, g_ref[0:1, :]).astype(BF16)
    pair = 2 * A_GROUP_DIM
    ss = jnp.zeros((tm, 1), F32)
    for gp in range(A_GROUPS // 2):
        cols = slice(gp * pair, (gp + 1) * pair)
        vg = _gelu(_dot(h, w_in_ref[:, A_WIDTH + gp * pair:A_WIDTH + (gp + 1) * pair]))
        ss = ss + jnp.sum(vg * vg, axis=-1, keepdims=True)
        v_scr[:, cols] = vg
    r = lax.rsqrt(ss * (1.0 / A_WIDTH) + NORM_EPS)
    acc = jnp.zeros(x.shape, F32)
    for gp in range(A_GROUPS // 2):
        cols = slice(gp * pair, (gp + 1) * pair)
        vn = (v_scr[:, cols] * r * vn_ref[:, cols]).astype(BF16)
        u = _gelu(_dot(h, w_in_ref[:, cols]))
        gated = []
        for c in range(tm // CHUNK):
            rows = slice(c * CHUNK, (c + 1) * CHUNK)
            halves = []
            for k in range(2):
                g = 2 * gp + k
                gcols = slice(k * A_GROUP_DIM, (k + 1) * A_GROUP_DIM)
                sv = _dot(ws_ref[g], vn[rows, gcols]) + bs_ref[:, g:g + 1]
                halves.append((u[rows, gcols] * sv).astype(BF16))
            gated.append(jnp.concatenate(halves, axis=1))
        acc = acc + _dot(jnp.concatenate(gated, axis=0), w_out_ref[cols, :])
    o_ref[...] = x + _rms(acc, g_ref[1:2, :])


def _gmlp_call(x, gains, w_in, v_norm, w_s, b_s_t, w_out, *, tm=512):
    t, d = x.shape
    row_spec = pl.BlockSpec((tm, d), lambda i: (i, 0))
    return pl.pallas_call(
        _gmlp_kernel,
        grid=(t // tm,),
        in_specs=[row_spec, _const_spec(gains), _const_spec(w_in), _const_spec(v_norm),
                  _const_spec(w_s), _const_spec(b_s_t), _const_spec(w_out)],
        out_specs=row_spec,
        out_shape=jax.ShapeDtypeStruct(x.shape, x.dtype),
        scratch_shapes=[pltpu.VMEM((tm, A_WIDTH), F32)],
        compiler_params=_params(1),
        name="gmlp",
    )(x, _raw(gains), _raw(w_in), _raw(v_norm), _raw(w_s), _raw(b_s_t), _raw(w_out))


_QUARTER = AXIS_DIM // 2
_HEAD_PERM = tuple(list(range(0, _QUARTER)) + list(range(2 * _QUARTER, 3 * _QUARTER))
                   + list(range(_QUARTER, 2 * _QUARTER)) + list(range(3 * _QUARTER, 4 * _QUARTER)))


def _permute_qk_heads(w_qkv, q_norm, k_norm):
    n_rot = (N_HEADS + N_KV_HEADS) * HEAD_DIM
    perm = jnp.asarray(_HEAD_PERM, jnp.int32)
    cols = (jnp.arange(n_rot, dtype=jnp.int32).reshape(-1, HEAD_DIM)[:, perm]).reshape(-1)
    cols = jnp.concatenate([cols, jnp.arange(n_rot, w_qkv.shape[-1], dtype=jnp.int32)])
    return jnp.take(w_qkv, cols, axis=-1), jnp.take(q_norm, perm, axis=-1), jnp.take(k_norm, perm, axis=-1)


def _rope_tables(s):
    pos = jnp.arange(s, dtype=jnp.int32)
    row_pos = (pos // GRID_W).astype(F32)
    col_pos = (pos % GRID_W).astype(F32)
    inv_freq = ROPE_THETA ** (-jnp.arange(0, AXIS_DIM, 2, dtype=F32) / AXIS_DIM)
    ang_row = row_pos[:, None] * inv_freq[None, :]
    ang_col = col_pos[:, None] * inv_freq[None, :]
    cos = jnp.concatenate([jnp.cos(ang_row), jnp.cos(ang_col)] * 2, axis=-1)
    sin = jnp.concatenate([-jnp.sin(ang_row), -jnp.sin(ang_col), jnp.sin(ang_row), jnp.sin(ang_col)], axis=-1)
    return cos, sin


def _qkv_kernel(x_ref, g_ref, w_ref, qn_ref, kn_ref, cos_ref, sin_ref, q_ref, k_ref, v_ref):
    h = _rms(x_ref[...], g_ref[0:1, :]).astype(BF16)
    cos = cos_ref[...]
    sin = sin_ref[...]
    scale = HEAD_DIM ** -0.5 * math.log2(math.e)

    pair = 2 * HEAD_DIM
    blk = lambda axis: lax.broadcasted_iota(jnp.int32, (pair, pair), axis) // HEAD_DIM
    head_ones = (blk(0) == blk(1)).astype(BF16)

    def norm_rope(y, gain):
        sq = y * y
        hi = sq.astype(BF16)
        lo = (sq - hi.astype(F32)).astype(BF16)
        ms = (_dot(hi, head_ones) + _dot(lo, head_ones)) * (1.0 / HEAD_DIM)
        heads = []
        for k in range(2):
            cols = slice(k * HEAD_DIM, (k + 1) * HEAD_DIM)
            t = y[:, cols] * lax.rsqrt(ms[:, cols] + NORM_EPS) * gain
            heads.append(t * cos + pltpu.roll(t, HEAD_DIM // 2, 1) * sin)
        return heads

    n_q_tiles = N_HEADS // 2
    n_k_tiles = N_KV_HEADS // 2
    wide = 2 * pair
    for blk_i in range(w_ref.shape[1] // wide):
        yy = _dot(h, w_ref[:, blk_i * wide:(blk_i + 1) * wide])
        for half in range(2):
            tile = 2 * blk_i + half
            cols = slice(tile * pair, (tile + 1) * pair)
            y = yy[:, half * pair:(half + 1) * pair]
            if tile < n_q_tiles:
                q_ref[:, cols] = (jnp.concatenate(norm_rope(y, qn_ref[...]), axis=1) * scale).astype(BF16)
            elif tile < n_q_tiles + n_k_tiles:
                kcols = slice((tile - n_q_tiles) * pair, (tile - n_q_tiles + 1) * pair)
                k_ref[:, kcols] = jnp.concatenate(norm_rope(y, kn_ref[...]), axis=1).astype(BF16)
            else:
                vcols = slice((tile - n_q_tiles - n_k_tiles) * pair, (tile - n_q_tiles - n_k_tiles + 1) * pair)
                v_ref[:, vcols] = y.astype(BF16)


def _qkv_call(x, gains, w_qkv, q_norm, k_norm, cos, sin, *, tm=512):
    t, d = x.shape
    s = cos.shape[0]
    blocks_per_seq = s // tm
    row = lambda i: (i, 0)
    pos = lambda i: (i % blocks_per_seq, 0)
    kv_w = N_KV_HEADS * HEAD_DIM
    return pl.pallas_call(
        _qkv_kernel,
        grid=(t // tm,),
        in_specs=[pl.BlockSpec((tm, d), row), _const_spec(gains), _const_spec(w_qkv),
                  _const_spec(q_norm), _const_spec(k_norm),
                  pl.BlockSpec((tm, HEAD_DIM), pos), pl.BlockSpec((tm, HEAD_DIM), pos)],
        out_specs=[pl.BlockSpec((tm, d), row), pl.BlockSpec((tm, kv_w), row), pl.BlockSpec((tm, kv_w), row)],
        out_shape=[jax.ShapeDtypeStruct((t, d), BF16), jax.ShapeDtypeStruct((t, kv_w), BF16),
                   jax.ShapeDtypeStruct((t, kv_w), BF16)],
        compiler_params=_params(1),
        name="qkv",
    )(x, _raw(gains), _raw(w_qkv), _raw(q_norm), _raw(k_norm), cos, sin)


def _attn_kernel(q_ref, k_ref, v_ref, o_ref, *, tk, unroll):
    tq = q_ref.shape[1]
    s = k_ref.shape[1]
    groups = k_ref.shape[2] // HEAD_DIM
    rows = Q_PER_KV * tq
    ones = jnp.ones((tk, HEAD_DIM), BF16)
    head_cols = lambda hd: slice(hd * HEAD_DIM, (hd + 1) * HEAD_DIM)
    qs = [jnp.concatenate([q_ref[0, :, head_cols(g * Q_PER_KV + hd)] for hd in range(Q_PER_KV)], axis=0)
          for g in range(groups)]

    def body(j, carry):
        start = pl.multiple_of(j * tk, tk)
        out = []
        for g in range(groups):
            m, acc = carry[g]
            kc = k_ref[0, pl.ds(start, tk), head_cols(g)]
            vc = jnp.concatenate([v_ref[0, pl.ds(start, tk), head_cols(g)], ones], axis=1)
            sc = lax.dot_general(qs[g], kc, (((1,), (1,)), ((), ())), preferred_element_type=F32)
            m_new = jnp.maximum(m, jnp.max(sc, axis=-1, keepdims=True))
            alpha = jnp.exp2(m - m_new)
            p = jnp.exp2(sc - m_new)
            out.append((m_new, alpha * acc + _dot(p.astype(BF16), vc)))
        return tuple(out)

    init = (jnp.full((rows, 1), -jnp.inf, F32), jnp.zeros((rows, 2 * HEAD_DIM), F32))
    final = lax.fori_loop(0, s // tk, body, (init,) * groups, unroll=unroll)
    for g in range(groups):
        acc = final[g][1]
        o = acc[:, :HEAD_DIM] / acc[:, HEAD_DIM:]
        for hd in range(Q_PER_KV):
            o_ref[0, :, head_cols(g * Q_PER_KV + hd)] = o[hd * tq:(hd + 1) * tq, :].astype(BF16)


ATTN_TQ = 256
ATTN_TK = 512
ATTN_BODY_TRIPS = 8


def _attn_config(s):
    trips = s // ATTN_TK
    groups = N_KV_HEADS if trips * N_KV_HEADS <= ATTN_BODY_TRIPS else 1
    return dict(tq=ATTN_TQ, tk=ATTN_TK, unroll=min(trips, ATTN_BODY_TRIPS // groups), groups=groups)


def _attn_call(q, k, v, *, tq, tk, unroll, groups):
    b, s, d = q.shape
    gw = groups * Q_PER_KV * HEAD_DIM
    q_spec = pl.BlockSpec((1, tq, gw), lambda bi, g, i: (bi, i, g))
    kv_spec = pl.BlockSpec((1, s, groups * HEAD_DIM), lambda bi, g, i: (bi, 0, g), pipeline_mode=pl.Buffered(1))
    return pl.pallas_call(
        functools.partial(_attn_kernel, tk=tk, unroll=unroll),
        grid=(b, N_KV_HEADS // groups, s // tq),
        in_specs=[q_spec, kv_spec, kv_spec],
        out_specs=q_spec,
        out_shape=jax.ShapeDtypeStruct(q.shape, BF16),
        compiler_params=_params(3),
        name="attn",
    )(q, k, v)


def _oproj_kernel(x_ref, a_ref, g_ref, w_ref, o_ref):
    y = _dot(a_ref[...], w_ref[...])
    o_ref[...] = x_ref[...] + _rms(y, g_ref[1:2, :])


def _oproj_call(x, a, gains, w, *, tm=512):
    t, d = x.shape
    row_spec = pl.BlockSpec((tm, d), lambda i: (i, 0))
    return pl.pallas_call(
        _oproj_kernel,
        grid=(t // tm,),
        in_specs=[row_spec, row_spec, _const_spec(gains), _const_spec(w)],
        out_specs=row_spec,
        out_shape=jax.ShapeDtypeStruct(x.shape, x.dtype),
        compiler_params=_params(1),
        name="oproj",
    )(x, a, _raw(gains), _raw(w))


def _dft_cos_sin(n):
    idx = jnp.arange(n, dtype=jnp.int32)
    ang = ((idx[:, None] * idx[None, :]) % n).astype(F32) * (2.0 * math.pi / n)
    return jnp.cos(ang), jnp.sin(ang)


def _fnet_tables(s):
    n1 = s // DFT_N2
    eye = jnp.eye(SUBLANES, dtype=F32)
    c_ch, s_ch = _dft_cos_sin(C_GROUP_DIM)
    chan = (jnp.concatenate([c_ch, s_ch], axis=1) * C_GROUP_DIM ** -0.5).astype(BF16)
    c1, s1 = _dft_cos_sin(n1)
    kc = jnp.kron(c1, eye)
    ks = jnp.kron(s1, eye)
    m_a = (jnp.block([[kc, -ks], [-ks, -kc]]) * s ** -0.5).astype(BF16)
    c2, s2 = _dft_cos_sin(DFT_N2)
    expand = lambda f: jnp.einsum("kn,ab->kabn", f, eye).reshape(DFT_N2 * SUBLANES, SUBLANES * DFT_N2)
    m_b = jnp.concatenate([expand(c2), expand(s2)], axis=1).astype(BF16)
    k1 = jnp.arange(n1, dtype=jnp.int32)
    n2 = jnp.arange(DFT_N2, dtype=jnp.int32)
    ang = ((k1[:, None] * n2[None, :]) % s).astype(F32) * (2.0 * math.pi / s)
    lanes = lambda t: jnp.broadcast_to(t.reshape(s, 1), (s, 128))
    return chan, m_a, m_b, lanes(jnp.cos(ang)), lanes(jnp.sin(ang))


def _fnet_a_kernel(x_ref, g_ref, chan_ref, m_ref, tc_ref, ts_ref, ar_ref, ai_ref):
    n1 = x_ref.shape[0]
    rows = n1 * SUBLANES
    x = x_ref[...].reshape(rows, D_MODEL)
    h = _rms(x, g_ref[0:1, :]).astype(BF16)
    zc, zs = [], []
    for g in range(C_GROUPS):
        z = _dot(h[:, g * C_GROUP_DIM:(g + 1) * C_GROUP_DIM], chan_ref[...])
        zc.append(z[:, :C_GROUP_DIM])
        zs.append(z[:, C_GROUP_DIM:])
    z = jnp.concatenate([jnp.concatenate(zc, axis=1), jnp.concatenate(zs, axis=1)], axis=0).astype(BF16)
    a = _dot(m_ref[...], z)
    a_re, a_im = a[:rows, :], a[rows:, :]
    tc = jnp.concatenate([tc_ref[...].reshape(rows, 128)] * (D_MODEL // 128), axis=1)
    ts = jnp.concatenate([ts_ref[...].reshape(rows, 128)] * (D_MODEL // 128), axis=1)
    ar_ref[...] = (a_re * tc + a_im * ts).reshape(ar_ref.shape)
    ai_ref[...] = (a_im * tc - a_re * ts).reshape(ai_ref.shape)


def _fnet_b_kernel(x_ref, ar_ref, ai_ref, g_ref, m_ref, w_ref, o_ref):
    rows = DFT_N2 * SUBLANES
    a = jnp.concatenate([ar_ref[...].reshape(rows, D_MODEL), ai_ref[...].reshape(rows, D_MODEL)],
                        axis=0).astype(BF16)
    f = _dot(m_ref[...], a).astype(BF16)
    y = _dot(f, w_ref[...])
    x = x_ref[...].reshape(rows, D_MODEL)
    o_ref[...] = (x + _rms(y, g_ref[1:2, :])).reshape(o_ref.shape)


def _fnet_call(x, gains, w_out, tables):
    b, s, d = x.shape
    chan, m_a, m_b, tw_c, tw_s = tables
    n1 = s // DFT_N2
    n2a = DFT_N2 // SUBLANES
    k1a = n1 // SUBLANES
    xa = x.reshape(b, n1, n2a, SUBLANES, d)
    a_spec = pl.BlockSpec((None, n1, None, SUBLANES, d), lambda bi, j: (bi, 0, j, 0, 0))
    tw_spec = pl.BlockSpec((n1, None, SUBLANES, 128), lambda bi, j: (0, j, 0, 0))
    tw_shape = (n1, n2a, SUBLANES, 128)
    a_re, a_im = pl.pallas_call(
        _fnet_a_kernel,
        grid=(b, n2a),
        in_specs=[a_spec, _const_spec(gains), _const_spec(chan), _const_spec(m_a), tw_spec, tw_spec],
        out_specs=[a_spec, a_spec],
        out_shape=[jax.ShapeDtypeStruct(xa.shape, F32)] * 2,
        compiler_params=_params(2),
        name="fnet_a",
    )(xa, _raw(gains), chan, m_a, tw_c.reshape(tw_shape), tw_s.reshape(tw_shape))
    xb = x.reshape(b, DFT_N2, k1a, SUBLANES, d)
    in_spec = pl.BlockSpec((None, SUBLANES, n2a, SUBLANES, d), lambda bi, j: (bi, j, 0, 0, 0))
    x_spec = pl.BlockSpec((None, DFT_N2, None, SUBLANES, d), lambda bi, j: (bi, 0, j, 0, 0))
    out = pl.pallas_call(
        _fnet_b_kernel,
        grid=(b, k1a),
        in_specs=[x_spec, in_spec, in_spec, _const_spec(gains), _const_spec(m_b), _const_spec(w_out)],
        out_specs=x_spec,
        out_shape=jax.ShapeDtypeStruct(xb.shape, F32),
        compiler_params=_params(2),
        name="fnet_b",
    )(xb, a_re, a_im, _raw(gains), m_b, _raw(w_out))
    return out.reshape(b, s, d)


def _trunk(x, p):
    b, s, d = x.shape
    t = b * s
    for i in range(DEPTH):
        m, j = i % N_MIXERS, i // N_MIXERS
        g = _Layer(p["norm_gains"], i)
        mixer = lambda name: _Layer(p[name], j)
        if m == 0:
            x = _gmlp_call(x.reshape(t, d), g, mixer("a_w_in"), mixer("a_v_norm"), mixer("a_w_s"),
                           mixer("a_b_s_t"), mixer("a_w_out")).reshape(b, s, d)
        elif m == 1:
            cos, sin = _rope_tables(s)
            q, k, v = _qkv_call(x.reshape(t, d), g, mixer("b_w_qkv"), mixer("b_q_norm"), mixer("b_k_norm"),
                                cos, sin)
            kv_w = N_KV_HEADS * HEAD_DIM
            a = _attn_call(q.reshape(b, s, d), k.reshape(b, s, kv_w), v.reshape(b, s, kv_w), **_attn_config(s))
            x = _oproj_call(x.reshape(t, d), a.reshape(t, d), g, mixer("b_w_o")).reshape(b, s, d)
        else:
            x = _fnet_call(x, g, mixer("c_w_out"), _fnet_tables(s))
        x = _mlp_call(x.reshape(t, d), g, _Layer(p["mlp_w_in"], i), _Layer(p["mlp_w_out"], i)).reshape(b, s, d)
    return x


def kernel(x_prompt, x_sample, norm_gains, a_w_in, a_v_norm, a_w_s, a_b_s, a_w_out,
           b_w_qkv, b_q_norm, b_k_norm, b_w_o, c_w_out, mlp_w_in, mlp_w_out):
    b_w_qkv, b_q_norm, b_k_norm = _permute_qk_heads(b_w_qkv, b_q_norm, b_k_norm)
    p = {
        "norm_gains": norm_gains,
        "a_w_in": a_w_in.astype(BF16),
        "a_v_norm": a_v_norm[:, None, :],
        "a_w_s": a_w_s.astype(BF16),
        "a_b_s_t": jnp.swapaxes(a_b_s, 1, 2),
        "a_w_out": a_w_out.astype(BF16),
        "b_w_qkv": b_w_qkv.astype(BF16),
        "b_q_norm": b_q_norm[:, None, :],
        "b_k_norm": b_k_norm[:, None, :],
        "b_w_o": b_w_o.astype(BF16),
        "c_w_out": c_w_out.astype(BF16),
        "mlp_w_in": mlp_w_in.astype(BF16),
        "mlp_w_out": mlp_w_out.astype(BF16),
    }
    return (_trunk(x_prompt, p), _trunk(x_sample, p))
```

```python
import functools
import math
from typing import NamedTuple

import jax
import jax.numpy as jnp
from jax import lax
from jax.experimental import pallas as pl
from jax.experimental.pallas import tpu as pltpu

D_MODEL = 1024
DEPTH = 4
N_MIXERS = 3
GRID_W = 64
NORM_EPS = 1e-6
CHUNK = 128
A_WIDTH = 3 * D_MODEL
A_GROUPS = 8
A_GROUP_DIM = A_WIDTH // A_GROUPS
HEAD_DIM = 128
N_HEADS = D_MODEL // HEAD_DIM
N_KV_HEADS = 2
Q_PER_KV = N_HEADS // N_KV_HEADS
AXIS_DIM = HEAD_DIM // 2
ROPE_THETA = 10000.0
C_GROUPS = 8
C_GROUP_DIM = D_MODEL // C_GROUPS
D_FF = 4 * D_MODEL

SUBLANES = 8
VMEM_LIMIT_BYTES = 56 * 1024 * 1024

DFT_N2 = 128

BF16 = jnp.bfloat16
F32 = jnp.float32


def _rms(x, g):
    return x * lax.rsqrt(jnp.mean(x * x, axis=-1, keepdims=True) + NORM_EPS) * g


def _gelu(x):
    return 0.5 * x * (1.0 + lax.erf(x * math.sqrt(0.5)))


def _dot(a, b):
    return jnp.dot(a, b, preferred_element_type=F32)


class _Layer(NamedTuple):
    stacked: jax.Array
    index: int


def _const_spec(operand):
    if isinstance(operand, _Layer):
        shape = (None,) + operand.stacked.shape[1:]
        index = (operand.index,) + (0,) * (len(shape) - 1)
    else:
        shape, index = operand.shape, (0,) * operand.ndim
    return pl.BlockSpec(shape, lambda *_: index, pipeline_mode=pl.Buffered(1))


def _raw(operand):
    return operand.stacked if isinstance(operand, _Layer) else operand


def _params(n_axes):
    return pltpu.CompilerParams(dimension_semantics=("parallel",) * n_axes,
                                vmem_limit_bytes=VMEM_LIMIT_BYTES)


def _mlp_kernel(x_ref, g_ref, w1_ref, w2_ref, o_ref, *, ff_chunk):
    x = x_ref[...]
    h = _rms(x, g_ref[2:3, :]).astype(BF16)
    acc = jnp.zeros(x.shape, F32)
    for c in range(D_FF // ff_chunk):
        cols = slice(c * ff_chunk, (c + 1) * ff_chunk)
        a = _dot(h, w1_ref[:, cols])
        a = jnp.square(jnp.maximum(a, 0.0)).astype(BF16)
        acc = acc + _dot(a, w2_ref[cols, :])
    o_ref[...] = x + _rms(acc, g_ref[3:4, :])


def _mlp_call(x, gains, w1, w2, *, tm=512, ff_chunk=1024):
    t, d = x.shape
    row_spec = pl.BlockSpec((tm, d), lambda i: (i, 0))
    return pl.pallas_call(
        functools.partial(_mlp_kernel, ff_chunk=ff_chunk),
        grid=(t // tm,),
        in_specs=[row_spec, _const_spec(gains), _const_spec(w1), _const_spec(w2)],
        out_specs=row_spec,
        out_shape=jax.ShapeDtypeStruct(x.shape, x.dtype),
        compiler_params=_params(1),
        name="mlp",
    )(x, _raw(gains), _raw(w1), _raw(w2))


def _gmlp_kernel(x_ref, g_ref, w_in_ref, vn_ref, ws_ref, bs_ref, w_out_ref, o_ref, v_scr):
    x = x_ref[...]
    tm = x.shape[0]
    h = _rms(x, g_ref[0:1, :]).astype(BF16)
    pair = 2 * A_GROUP_DIM
    ss = jnp.zeros((tm, 1), F32)
    for gp in range(A_GROUPS // 2):
        cols = slice(gp * pair, (gp + 1) * pair)
        vg = _gelu(_dot(h, w_in_ref[:, A_WIDTH + gp * pair:A_WIDTH + (gp + 1) * pair]))
        ss = ss + jnp.sum(vg * vg, axis=-1, keepdims=True)
        v_scr[:, cols] = vg
    r = lax.rsqrt(ss * (1.0 / A_WIDTH) + NORM_EPS)
    acc = jnp.zeros(x.shape, F32)
    for gp in range(A_GROUPS // 2):
        cols = slice(gp * pair, (gp + 1) * pair)
        vn = (v_scr[:, cols] * r * vn_ref[:, cols]).astype(BF16)
        u = _gelu(_dot(h, w_in_ref[:, cols]))
        gated = []
        for c in range(tm // CHUNK):
            rows = slice(c * CHUNK, (c + 1) * CHUNK)
            halves = []
            for k in range(2):
                g = 2 * gp + k
                gcols = slice(k * A_GROUP_DIM, (k + 1) * A_GROUP_DIM)
                sv = _dot(ws_ref[g], vn[rows, gcols]) + bs_ref[:, g:g + 1]
                halves.append((u[rows, gcols] * sv).astype(BF16))
            gated.append(jnp.concatenate(halves, axis=1))
        acc = acc + _dot(jnp.concatenate(gated, axis=0), w_out_ref[cols, :])
    o_ref[...] = x + _rms(acc, g_ref[1:2, :])


def _gmlp_call(x, gains, w_in, v_norm, w_s, b_s_t, w_out, *, tm=512):
    t, d = x.shape
    row_spec = pl.BlockSpec((tm, d), lambda i: (i, 0))
    return pl.pallas_call(
        _gmlp_kernel,
        grid=(t // tm,),
        in_specs=[row_spec, _const_spec(gains), _const_spec(w_in), _const_spec(v_norm),
                  _const_spec(w_s), _const_spec(b_s_t), _const_spec(w_out)],
        out_specs=row_spec,
        out_shape=jax.ShapeDtypeStruct(x.shape, x.dtype),
        scratch_shapes=[pltpu.VMEM((tm, A_WIDTH), F32)],
        compiler_params=_params(1),
        name="gmlp",
    )(x, _raw(gains), _raw(w_in), _raw(v_norm), _raw(w_s), _raw(b_s_t), _raw(w_out))


_QUARTER = AXIS_DIM // 2
_HEAD_PERM = tuple(list(range(0, _QUARTER)) + list(range(2 * _QUARTER, 3 * _QUARTER))
                   + list(range(_QUARTER, 2 * _QUARTER)) + list(range(3 * _QUARTER, 4 * _QUARTER)))


def _permute_qk_heads(w_qkv, q_norm, k_norm):
    n_rot = (N_HEADS + N_KV_HEADS) * HEAD_DIM
    perm = jnp.asarray(_HEAD_PERM, jnp.int32)
    cols = (jnp.arange(n_rot, dtype=jnp.int32).reshape(-1, HEAD_DIM)[:, perm]).reshape(-1)
    cols = jnp.concatenate([cols, jnp.arange(n_rot, w_qkv.shape[-1], dtype=jnp.int32)])
    return jnp.take(w_qkv, cols, axis=-1), jnp.take(q_norm, perm, axis=-1), jnp.take(k_norm, perm, axis=-1)


def _rope_tables(s):
    pos = jnp.arange(s, dtype=jnp.int32)
    row_pos = (pos // GRID_W).astype(F32)
    col_pos = (pos % GRID_W).astype(F32)
    inv_freq = ROPE_THETA ** (-jnp.arange(0, AXIS_DIM, 2, dtype=F32) / AXIS_DIM)
    ang_row = row_pos[:, None] * inv_freq[None, :]
    ang_col = col_pos[:, None] * inv_freq[None, :]
    cos = jnp.concatenate([jnp.cos(ang_row), jnp.cos(ang_col)] * 2, axis=-1)
    sin = jnp.concatenate([-jnp.sin(ang_row), -jnp.sin(ang_col), jnp.sin(ang_row), jnp.sin(ang_col)], axis=-1)
    return cos, sin


def _qkv_kernel(x_ref, g_ref, w_ref, qn_ref, kn_ref, cos_ref, sin_ref, q_ref, k_ref, v_ref):
    h = _rms(x_ref[...], g_ref[0:1, :]).astype(BF16)
    cos = cos_ref[...]
    sin = sin_ref[...]
    scale = HEAD_DIM ** -0.5 * math.log2(math.e)

    pair = 2 * HEAD_DIM
    blk = lambda axis: lax.broadcasted_iota(jnp.int32, (pair, pair), axis) // HEAD_DIM
    head_ones = (blk(0) == blk(1)).astype(BF16)

    def norm_rope(y, gain):
        sq = y * y
        hi = sq.astype(BF16)
        lo = (sq - hi.astype(F32)).astype(BF16)
        ms = (_dot(hi, head_ones) + _dot(lo, head_ones)) * (1.0 / HEAD_DIM)
        heads = []
        for k in range(2):
            cols = slice(k * HEAD_DIM, (k + 1) * HEAD_DIM)
            t = y[:, cols] * lax.rsqrt(ms[:, cols] + NORM_EPS) * gain
            heads.append(t * cos + pltpu.roll(t, HEAD_DIM // 2, 1) * sin)
        return heads

    n_q_tiles = N_HEADS // 2
    n_k_tiles = N_KV_HEADS // 2
    wide = 2 * pair
    for blk_i in range(w_ref.shape[1] // wide):
        yy = _dot(h, w_ref[:, blk_i * wide:(blk_i + 1) * wide])
        for half in range(2):
            tile = 2 * blk_i + half
            cols = slice(tile * pair, (tile + 1) * pair)
            y = yy[:, half * pair:(half + 1) * pair]
            if tile < n_q_tiles:
                q_ref[:, cols] = (jnp.concatenate(norm_rope(y, qn_ref[...]), axis=1) * scale).astype(BF16)
            elif tile < n_q_tiles + n_k_tiles:
                kcols = slice((tile - n_q_tiles) * pair, (tile - n_q_tiles + 1) * pair)
                k_ref[:, kcols] = jnp.concatenate(norm_rope(y, kn_ref[...]), axis=1).astype(BF16)
            else:
                vcols = slice((tile - n_q_tiles - n_k_tiles) * pair, (tile - n_q_tiles - n_k_tiles + 1) * pair)
                v_ref[:, vcols] = y.astype(BF16)


def _qkv_call(x, gains, w_qkv, q_norm, k_norm, cos, sin, *, tm=512):
    t, d = x.shape
    s = cos.shape[0]
    blocks_per_seq = s // tm
    row = lambda i: (i, 0)
    pos = lambda i: (i % blocks_per_seq, 0)
    kv_w = N_KV_HEADS * HEAD_DIM
    return pl.pallas_call(
        _qkv_kernel,
        grid=(t // tm,),
        in_specs=[pl.BlockSpec((tm, d), row), _const_spec(gains), _const_spec(w_qkv),
                  _const_spec(q_norm), _const_spec(k_norm),
                  pl.BlockSpec((tm, HEAD_DIM), pos), pl.BlockSpec((tm, HEAD_DIM), pos)],
        out_specs=[pl.BlockSpec((tm, d), row), pl.BlockSpec((tm, kv_w), row), pl.BlockSpec((tm, kv_w), row)],
        out_shape=[jax.ShapeDtypeStruct((t, d), BF16), jax.ShapeDtypeStruct((t, kv_w), BF16),
                   jax.ShapeDtypeStruct((t, kv_w), BF16)],
        compiler_params=_params(1),
        name="qkv",
    )(x, _raw(gains), _raw(w_qkv), _raw(q_norm), _raw(k_norm), cos, sin)


def _attn_kernel(q_ref, k_ref, v_ref, o_ref, *, tk, unroll):
    tq = q_ref.shape[1]
    s = k_ref.shape[1]
    groups = k_ref.shape[2] // HEAD_DIM
    rows = Q_PER_KV * tq
    ones = jnp.ones((tk, HEAD_DIM), BF16)
    head_cols = lambda hd: slice(hd * HEAD_DIM, (hd + 1) * HEAD_DIM)
    qs = [jnp.concatenate([q_ref[0, :, head_cols(g * Q_PER_KV + hd)] for hd in range(Q_PER_KV)], axis=0)
          for g in range(groups)]

    def body(j, carry):
        start = pl.multiple_of(j * tk, tk)
        out = []
        for g in range(groups):
            m, acc = carry[g]
            kc = k_ref[0, pl.ds(start, tk), head_cols(g)]
            vc = jnp.concatenate([v_ref[0, pl.ds(start, tk), head_cols(g)], ones], axis=1)
            sc = lax.dot_general(qs[g], kc, (((1,), (1,)), ((), ())), preferred_element_type=F32)
            m_new = jnp.maximum(m, jnp.max(sc, axis=-1, keepdims=True))
            alpha = jnp.exp2(m - m_new)
            p = jnp.exp2(sc - m_new)
            out.append((m_new, alpha * acc + _dot(p.astype(BF16), vc)))
        return tuple(out)

    init = (jnp.full((rows, 1), -jnp.inf, F32), jnp.zeros((rows, 2 * HEAD_DIM), F32))
    final = lax.fori_loop(0, s // tk, body, (init,) * groups, unroll=unroll)
    for g in range(groups):
        acc = final[g][1]
        o = acc[:, :HEAD_DIM] / acc[:, HEAD_DIM:]
        for hd in range(Q_PER_KV):
            o_ref[0, :, head_cols(g * Q_PER_KV + hd)] = o[hd * tq:(hd + 1) * tq, :].astype(BF16)


ATTN_TQ = 256
ATTN_TK = 512
ATTN_BODY_TRIPS = 8


def _attn_config(s):
    trips = s // ATTN_TK
    groups = N_KV_HEADS if trips * N_KV_HEADS <= ATTN_BODY_TRIPS else 1
    return dict(tq=ATTN_TQ, tk=ATTN_TK, unroll=min(trips, ATTN_BODY_TRIPS // groups), groups=groups)


def _attn_call(q, k, v, *, tq, tk, unroll, groups):
    b, s, d = q.shape
    gw = groups * Q_PER_KV * HEAD_DIM
    q_spec = pl.BlockSpec((1, tq, gw), lambda bi, g, i: (bi, i, g))
    kv_spec = pl.BlockSpec((1, s, groups * HEAD_DIM), lambda bi, g, i: (bi, 0, g), pipeline_mode=pl.Buffered(1))
    return pl.pallas_call(
        functools.partial(_attn_kernel, tk=tk, unroll=unroll),
        grid=(b, N_KV_HEADS // groups, s // tq),
        in_specs=[q_spec, kv_spec, kv_spec],
        out_specs=q_spec,
        out_shape=jax.ShapeDtypeStruct(q.shape, BF16),
        compiler_params=_params(3),
        name="attn",
    )(q, k, v)


def _oproj_kernel(x_ref, a_ref, g_ref, w_ref, o_ref):
    y = _dot(a_ref[...], w_ref[...])
    o_ref[...] = x_ref[...] + _rms(y, g_ref[1:2, :])


def _oproj_call(x, a, gains, w, *, tm=512):
    t, d = x.shape
    row_spec = pl.BlockSpec((tm, d), lambda i: (i, 0))
    return pl.pallas_call(
        _oproj_kernel,
        grid=(t // tm,),
        in_specs=[row_spec, row_spec, _const_spec(gains), _const_spec(w)],
        out_specs=row_spec,
        out_shape=jax.ShapeDtypeStruct(x.shape, x.dtype),
        compiler_params=_params(1),
        name="oproj",
    )(x, a, _raw(gains), _raw(w))


def _dft_cos_sin(n):
    idx = jnp.arange(n, dtype=jnp.int32)
    ang = ((idx[:, None] * idx[None, :]) % n).astype(F32) * (2.0 * math.pi / n)
    return jnp.cos(ang), jnp.sin(ang)


FNET_KRON_MAX_N1 = 16
FNET_A_ROWS = 1024


def _fnet_plan(s):
    n1 = s // DFT_N2
    kron = n1 <= FNET_KRON_MAX_N1
    n2a_per_step = max(1, FNET_A_ROWS // (n1 * SUBLANES)) if kron else 1
    return n1, kron, n2a_per_step


def _fnet_tables(s):
    n1, kron, _ = _fnet_plan(s)
    c_ch, s_ch = _dft_cos_sin(C_GROUP_DIM)
    chan = (jnp.concatenate([c_ch, s_ch], axis=1) * C_GROUP_DIM ** -0.5).astype(BF16)
    c1, s1 = _dft_cos_sin(n1)
    if kron:
        eye = jnp.eye(SUBLANES, dtype=F32)
        c1, s1 = jnp.kron(c1, eye), jnp.kron(s1, eye)
    m_a = (jnp.block([[c1, -s1], [-s1, -c1]]) * s ** -0.5).astype(BF16)
    c2, s2 = _dft_cos_sin(DFT_N2)
    m_b = jnp.concatenate([c2, s2], axis=1).astype(BF16)
    k1 = jnp.arange(n1, dtype=jnp.int32)
    n2 = jnp.arange(DFT_N2, dtype=jnp.int32)
    ang = ((k1[:, None] * n2[None, :]) % s).astype(F32) * (2.0 * math.pi / s)
    n2a = DFT_N2 // SUBLANES
    if kron:
        lanes = lambda t: jnp.broadcast_to(t.reshape(n1, n2a, SUBLANES, 1), (n1, n2a, SUBLANES, 128))
    else:
        lanes = lambda t: jnp.broadcast_to(t.T.reshape(n2a, SUBLANES, n1, 1), (n2a, SUBLANES, n1, 128))
    return chan, m_a, m_b, lanes(jnp.cos(ang)), lanes(jnp.sin(ang))


def _fnet_a_kernel(x_ref, g_ref, chan_ref, m_ref, tc_ref, ts_ref, ar_ref, ai_ref, *scratch, kron):
    n1, steps = x_ref.shape[0], x_ref.shape[1]
    rows = n1 * SUBLANES
    panels = D_MODEL // 128
    widen = lambda t: jnp.concatenate([t] * panels, axis=1)
    for a_i in range(steps):
        x = x_ref[:, a_i].reshape(rows, D_MODEL)
        h = _rms(x, g_ref[0:1, :]).astype(BF16)
        zc, zs = [], []
        for g in range(C_GROUPS):
            z = _dot(h[:, g * C_GROUP_DIM:(g + 1) * C_GROUP_DIM], chan_ref[...])
            zc.append(z[:, :C_GROUP_DIM])
            zs.append(z[:, C_GROUP_DIM:])
        if kron:
            z = jnp.concatenate([jnp.concatenate(zc, axis=1), jnp.concatenate(zs, axis=1)], axis=0).astype(BF16)
            a = _dot(m_ref[...], z)
            a_re, a_im = a[:rows, :], a[rows:, :]
            tc = widen(tc_ref[:, a_i].reshape(rows, 128))
            ts = widen(ts_ref[:, a_i].reshape(rows, 128))
            ar_ref[:, a_i] = (a_re * tc + a_im * ts).reshape(n1, SUBLANES, D_MODEL)
            ai_ref[:, a_i] = (a_im * tc - a_re * ts).reshape(n1, SUBLANES, D_MODEL)
        else:
            z_scr, a_scr = scratch
            for c in range(panels):
                z_scr[c] = zc[c]
                z_scr[panels + c] = zs[c]
            for n2b in range(SUBLANES):
                take = lambda p: z_scr[p, pl.ds(n2b, n1, stride=SUBLANES), :]
                z = jnp.concatenate([jnp.concatenate([take(c) for c in range(panels)], axis=1),
                                     jnp.concatenate([take(panels + c) for c in range(panels)], axis=1)],
                                    axis=0).astype(BF16)
                a = _dot(m_ref[...], z)
                a_re, a_im = a[:n1, :], a[n1:, :]
                tc = widen(tc_ref[a_i, n2b])
                ts = widen(ts_ref[a_i, n2b])
                o_re = a_re * tc + a_im * ts
                o_im = a_im * tc - a_re * ts
                for c in range(panels):
                    a_scr[c, pl.ds(n2b, n1, stride=SUBLANES), :] = o_re[:, c * 128:(c + 1) * 128]
                    a_scr[panels + c, pl.ds(n2b, n1, stride=SUBLANES), :] = o_im[:, c * 128:(c + 1) * 128]
            gather = lambda base: jnp.concatenate([a_scr[base + c] for c in range(panels)], axis=1)
            ar_ref[:, a_i] = gather(0).reshape(n1, SUBLANES, D_MODEL)
            ai_ref[:, a_i] = gather(panels).reshape(n1, SUBLANES, D_MODEL)


def _fnet_b_kernel(x_ref, ar_ref, ai_ref, g_ref, cs_ref, w_ref, o_ref, f_scr):
    rows = DFT_N2 * SUBLANES
    for k1b in range(SUBLANES):
        a = jnp.concatenate([ar_ref[k1b].reshape(DFT_N2, D_MODEL), ai_ref[k1b].reshape(DFT_N2, D_MODEL)],
                            axis=0).astype(BF16)
        f = _dot(cs_ref[...], a)
        for c in range(f_scr.shape[0]):
            f_scr[c, pl.ds(k1b, DFT_N2, stride=SUBLANES), :] = f[:, c * 128:(c + 1) * 128]
    f = jnp.concatenate([f_scr[c] for c in range(f_scr.shape[0])], axis=1).astype(BF16)
    y = _dot(f, w_ref[...])
    x = x_ref[...].reshape(rows, D_MODEL)
    o_ref[...] = (x + _rms(y, g_ref[1:2, :])).reshape(o_ref.shape)


def _fnet_call(x, gains, w_out, tables):
    b, s, d = x.shape
    chan, m_a, m_b, tw_c, tw_s = tables
    n1, kron, nb = _fnet_plan(s)
    n2a = DFT_N2 // SUBLANES
    k1a = n1 // SUBLANES
    xa = x.reshape(b, n1, n2a, SUBLANES, d)
    a_spec = pl.BlockSpec((None, n1, nb, SUBLANES, d), lambda bi, j: (bi, 0, j, 0, 0))
    if kron:
        tw_spec = pl.BlockSpec((n1, nb, SUBLANES, 128), lambda bi, j: (0, j, 0, 0))
        scratch = []
    else:
        tw_spec = pl.BlockSpec((nb, SUBLANES, n1, 128), lambda bi, j: (j, 0, 0, 0))
        scratch = [pltpu.VMEM((2 * d // 128, n1 * SUBLANES, 128), F32)] * 2
    a_re, a_im = pl.pallas_call(
        functools.partial(_fnet_a_kernel, kron=kron),
        grid=(b, n2a // nb),
        in_specs=[a_spec, _const_spec(gains), _const_spec(chan), _const_spec(m_a), tw_spec, tw_spec],
        out_specs=[a_spec, a_spec],
        out_shape=[jax.ShapeDtypeStruct(xa.shape, F32)] * 2,
        scratch_shapes=scratch,
        compiler_params=_params(2),
        name="fnet_a",
    )(xa, _raw(gains), chan, m_a, tw_c, tw_s)
    xb = x.reshape(b, DFT_N2, k1a, SUBLANES, d)
    in_spec = pl.BlockSpec((None, SUBLANES, n2a, SUBLANES, d), lambda bi, j: (bi, j, 0, 0, 0))
    x_spec = pl.BlockSpec((None, DFT_N2, None, SUBLANES, d), lambda bi, j: (bi, 0, j, 0, 0))
    out = pl.pallas_call(
        _fnet_b_kernel,
        grid=(b, k1a),
        in_specs=[x_spec, in_spec, in_spec, _const_spec(gains), _const_spec(m_b), _const_spec(w_out)],
        out_specs=x_spec,
        out_shape=jax.ShapeDtypeStruct(xb.shape, F32),
        scratch_shapes=[pltpu.VMEM((d // 128, DFT_N2 * SUBLANES, 128), F32)],
        compiler_params=_params(2),
        name="fnet_b",
    )(xb, a_re, a_im, _raw(gains), m_b, _raw(w_out))
    return out.reshape(b, s, d)


def _trunk(x, p):
    b, s, d = x.shape
    t = b * s
    for i in range(DEPTH):
        m, j = i % N_MIXERS, i // N_MIXERS
        g = _Layer(p["norm_gains"], i)
        mixer = lambda name: _Layer(p[name], j)
        if m == 0:
            x = _gmlp_call(x.reshape(t, d), g, mixer("a_w_in"), mixer("a_v_norm"), mixer("a_w_s"),
                           mixer("a_b_s_t"), mixer("a_w_out")).reshape(b, s, d)
        elif m == 1:
            cos, sin = _rope_tables(s)
            q, k, v = _qkv_call(x.reshape(t, d), g, mixer("b_w_qkv"), mixer("b_q_norm"), mixer("b_k_norm"),
                                cos, sin)
            kv_w = N_KV_HEADS * HEAD_DIM
            a = _attn_call(q.reshape(b, s, d), k.reshape(b, s, kv_w), v.reshape(b, s, kv_w), **_attn_config(s))
            x = _oproj_call(x.reshape(t, d), a.reshape(t, d), g, mixer("b_w_o")).reshape(b, s, d)
        else:
            x = _fnet_call(x, g, mixer("c_w_out"), _fnet_tables(s))
        x = _mlp_call(x.reshape(t, d), g, _Layer(p["mlp_w_in"], i), _Layer(p["mlp_w_out"], i)).reshape(b, s, d)
    return x


def kernel(x_prompt, x_sample, norm_gains, a_w_in, a_v_norm, a_w_s, a_b_s, a_w_out,
           b_w_qkv, b_q_norm, b_k_norm, b_w_o, c_w_out, mlp_w_in, mlp_w_out):
    b_w_qkv, b_q_norm, b_k_norm = _permute_qk_heads(b_w_qkv, b_q_norm, b_k_norm)
    p = {
        "norm_gains": norm_gains,
        "a_w_in": a_w_in.astype(BF16),
        "a_v_norm": a_v_norm[:, None, :],
        "a_w_s": a_w_s.astype(BF16),
        "a_b_s_t": jnp.swapaxes(a_b_s, 1, 2),
        "a_w_out": a_w_out.astype(BF16),
        "b_w_qkv": b_w_qkv.astype(BF16),
        "b_q_norm": b_q_norm[:, None, :],
        "b_k_norm": b_k_norm[:, None, :],
        "b_w_o": b_w_o.astype(BF16),
        "c_w_out": c_w_out.astype(BF16),
        "mlp_w_in": mlp_w_in.astype(BF16),
        "mlp_w_out": mlp_w_out.astype(BF16),
    }
    return (_trunk(x_prompt, p), _trunk(x_sample, p))
```

```python
import functools
import math
from typing import NamedTuple

import jax
import jax.numpy as jnp
from jax import lax
from jax.experimental import pallas as pl
from jax.experimental.pallas import tpu as pltpu

D_MODEL = 1024
DEPTH = 4
N_MIXERS = 3
GRID_W = 64
NORM_EPS = 1e-6
CHUNK = 128
A_WIDTH = 3 * D_MODEL
A_GROUPS = 8
A_GROUP_DIM = A_WIDTH // A_GROUPS
HEAD_DIM = 128
N_HEADS = D_MODEL // HEAD_DIM
N_KV_HEADS = 2
Q_PER_KV = N_HEADS // N_KV_HEADS
AXIS_DIM = HEAD_DIM // 2
ROPE_THETA = 10000.0
C_GROUPS = 8
C_GROUP_DIM = D_MODEL // C_GROUPS
D_FF = 4 * D_MODEL

SUBLANES = 8
VMEM_LIMIT_BYTES = 56 * 1024 * 1024

DFT_N2 = 128

BF16 = jnp.bfloat16
F32 = jnp.float32


def _rms(x, g):
    return x * lax.rsqrt(jnp.mean(x * x, axis=-1, keepdims=True) + NORM_EPS) * g


def _gelu(x):
    return 0.5 * x * (1.0 + lax.erf(x * math.sqrt(0.5)))


def _dot(a, b):
    return jnp.dot(a, b, preferred_element_type=F32)


class _Layer(NamedTuple):
    stacked: jax.Array
    index: int


def _const_spec(operand):
    if isinstance(operand, _Layer):
        shape = (None,) + operand.stacked.shape[1:]
        index = (operand.index,) + (0,) * (len(shape) - 1)
    else:
        shape, index = operand.shape, (0,) * operand.ndim
    return pl.BlockSpec(shape, lambda *_: index, pipeline_mode=pl.Buffered(1))


def _raw(operand):
    return operand.stacked if isinstance(operand, _Layer) else operand


def _params(n_axes):
    return pltpu.CompilerParams(dimension_semantics=("parallel",) * n_axes,
                                vmem_limit_bytes=VMEM_LIMIT_BYTES)


def _mlp_kernel(x_ref, g_ref, w1_ref, w2_ref, o_ref, *, ff_chunk):
    x = x_ref[...]
    h = _rms(x, g_ref[2:3, :]).astype(BF16)
    acc = jnp.zeros(x.shape, F32)
    for c in range(D_FF // ff_chunk):
        cols = slice(c * ff_chunk, (c + 1) * ff_chunk)
        a = _dot(h, w1_ref[:, cols])
        a = jnp.square(jnp.maximum(a, 0.0)).astype(BF16)
        acc = acc + _dot(a, w2_ref[cols, :])
    o_ref[...] = x + _rms(acc, g_ref[3:4, :])


def _mlp_call(x, gains, w1, w2, *, tm=512, ff_chunk=1024):
    t, d = x.shape
    row_spec = pl.BlockSpec((tm, d), lambda i: (i, 0))
    return pl.pallas_call(
        functools.partial(_mlp_kernel, ff_chunk=ff_chunk),
        grid=(t // tm,),
        in_specs=[row_spec, _const_spec(gains), _const_spec(w1), _const_spec(w2)],
        out_specs=row_spec,
        out_shape=jax.ShapeDtypeStruct(x.shape, x.dtype),
        compiler_params=_params(1),
        name="mlp",
    )(x, _raw(gains), _raw(w1), _raw(w2))


def _gmlp_kernel(x_ref, g_ref, w_in_ref, vn_ref, ws_ref, bs_ref, w_out_ref, o_ref, v_scr):
    x = x_ref[...]
    tm = x.shape[0]
    h = _rms(x, g_ref[0:1, :]).astype(BF16)
    pair = 2 * A_GROUP_DIM
    ss = jnp.zeros((tm, 1), F32)
    for gp in range(A_GROUPS // 2):
        cols = slice(gp * pair, (gp + 1) * pair)
        vg = _gelu(_dot(h, w_in_ref[:, A_WIDTH + gp * pair:A_WIDTH + (gp + 1) * pair]))
        ss = ss + jnp.sum(vg * vg, axis=-1, keepdims=True)
        v_scr[:, cols] = vg
    r = lax.rsqrt(ss * (1.0 / A_WIDTH) + NORM_EPS)
    acc = jnp.zeros(x.shape, F32)
    for gp in range(A_GROUPS // 2):
        cols = slice(gp * pair, (gp + 1) * pair)
        vn = (v_scr[:, cols] * r * vn_ref[:, cols]).astype(BF16)
        u = _gelu(_dot(h, w_in_ref[:, cols]))
        gated = []
        for c in range(tm // CHUNK):
            rows = slice(c * CHUNK, (c + 1) * CHUNK)
            halves = []
            for k in range(2):
                g = 2 * gp + k
                gcols = slice(k * A_GROUP_DIM, (k + 1) * A_GROUP_DIM)
                sv = _dot(ws_ref[g], vn[rows, gcols]) + bs_ref[:, g:g + 1]
                halves.append((u[rows, gcols] * sv).astype(BF16))
            gated.append(jnp.concatenate(halves, axis=1))
        acc = acc + _dot(jnp.concatenate(gated, axis=0), w_out_ref[cols, :])
    o_ref[...] = x + _rms(acc, g_ref[1:2, :])


def _gmlp_call(x, gains, w_in, v_norm, w_s, b_s_t, w_out, *, tm=512):
    t, d = x.shape
    row_spec = pl.BlockSpec((tm, d), lambda i: (i, 0))
    return pl.pallas_call(
        _gmlp_kernel,
        grid=(t // tm,),
        in_specs=[row_spec, _const_spec(gains), _const_spec(w_in), _const_spec(v_norm),
                  _const_spec(w_s), _const_spec(b_s_t), _const_spec(w_out)],
        out_specs=row_spec,
        out_shape=jax.ShapeDtypeStruct(x.shape, x.dtype),
        scratch_shapes=[pltpu.VMEM((tm, A_WIDTH), F32)],
        compiler_params=_params(1),
        name="gmlp",
    )(x, _raw(gains), _raw(w_in), _raw(v_norm), _raw(w_s), _raw(b_s_t), _raw(w_out))


_QUARTER = AXIS_DIM // 2
_HEAD_PERM = tuple(list(range(0, _QUARTER)) + list(range(2 * _QUARTER, 3 * _QUARTER))
                   + list(range(_QUARTER, 2 * _QUARTER)) + list(range(3 * _QUARTER, 4 * _QUARTER)))


def _permute_qk_heads(w_qkv, q_norm, k_norm):
    n_rot = (N_HEADS + N_KV_HEADS) * HEAD_DIM
    perm = jnp.asarray(_HEAD_PERM, jnp.int32)
    cols = (jnp.arange(n_rot, dtype=jnp.int32).reshape(-1, HEAD_DIM)[:, perm]).reshape(-1)
    cols = jnp.concatenate([cols, jnp.arange(n_rot, w_qkv.shape[-1], dtype=jnp.int32)])
    return jnp.take(w_qkv, cols, axis=-1), jnp.take(q_norm, perm, axis=-1), jnp.take(k_norm, perm, axis=-1)


def _rope_tables(s):
    pos = jnp.arange(s, dtype=jnp.int32)
    row_pos = (pos // GRID_W).astype(F32)
    col_pos = (pos % GRID_W).astype(F32)
    inv_freq = ROPE_THETA ** (-jnp.arange(0, AXIS_DIM, 2, dtype=F32) / AXIS_DIM)
    ang_row = row_pos[:, None] * inv_freq[None, :]
    ang_col = col_pos[:, None] * inv_freq[None, :]
    cos = jnp.concatenate([jnp.cos(ang_row), jnp.cos(ang_col)] * 2, axis=-1)
    sin = jnp.concatenate([-jnp.sin(ang_row), -jnp.sin(ang_col), jnp.sin(ang_row), jnp.sin(ang_col)], axis=-1)
    return cos, sin


def _qkv_kernel(x_ref, g_ref, w_ref, qn_ref, kn_ref, cos_ref, sin_ref, q_ref, k_ref, v_ref):
    h = _rms(x_ref[...], g_ref[0:1, :]).astype(BF16)
    cos = cos_ref[...]
    sin = sin_ref[...]
    scale = HEAD_DIM ** -0.5 * math.log2(math.e)

    pair = 2 * HEAD_DIM
    blk = lambda axis: lax.broadcasted_iota(jnp.int32, (pair, pair), axis) // HEAD_DIM
    head_ones = (blk(0) == blk(1)).astype(BF16)

    def norm_rope(y, gain):
        sq = y * y
        hi = sq.astype(BF16)
        lo = (sq - hi.astype(F32)).astype(BF16)
        ms = (_dot(hi, head_ones) + _dot(lo, head_ones)) * (1.0 / HEAD_DIM)
        heads = []
        for k in range(2):
            cols = slice(k * HEAD_DIM, (k + 1) * HEAD_DIM)
            t = y[:, cols] * lax.rsqrt(ms[:, cols] + NORM_EPS) * gain
            heads.append(t * cos + pltpu.roll(t, HEAD_DIM // 2, 1) * sin)
        return heads

    n_q_tiles = N_HEADS // 2
    n_k_tiles = N_KV_HEADS // 2
    wide = 2 * pair
    for blk_i in range(w_ref.shape[1] // wide):
        yy = _dot(h, w_ref[:, blk_i * wide:(blk_i + 1) * wide])
        for half in range(2):
            tile = 2 * blk_i + half
            cols = slice(tile * pair, (tile + 1) * pair)
            y = yy[:, half * pair:(half + 1) * pair]
            if tile < n_q_tiles:
                q_ref[:, cols] = (jnp.concatenate(norm_rope(y, qn_ref[...]), axis=1) * scale).astype(BF16)
            elif tile < n_q_tiles + n_k_tiles:
                kcols = slice((tile - n_q_tiles) * pair, (tile - n_q_tiles + 1) * pair)
                k_ref[:, kcols] = jnp.concatenate(norm_rope(y, kn_ref[...]), axis=1).astype(BF16)
            else:
                vcols = slice((tile - n_q_tiles - n_k_tiles) * pair, (tile - n_q_tiles - n_k_tiles + 1) * pair)
                v_ref[:, vcols] = y.astype(BF16)


def _qkv_call(x, gains, w_qkv, q_norm, k_norm, cos, sin, *, tm=512):
    t, d = x.shape
    s = cos.shape[0]
    blocks_per_seq = s // tm
    row = lambda i: (i, 0)
    pos = lambda i: (i % blocks_per_seq, 0)
    kv_w = N_KV_HEADS * HEAD_DIM
    return pl.pallas_call(
        _qkv_kernel,
        grid=(t // tm,),
        in_specs=[pl.BlockSpec((tm, d), row), _const_spec(gains), _const_spec(w_qkv),
                  _const_spec(q_norm), _const_spec(k_norm),
                  pl.BlockSpec((tm, HEAD_DIM), pos), pl.BlockSpec((tm, HEAD_DIM), pos)],
        out_specs=[pl.BlockSpec((tm, d), row), pl.BlockSpec((tm, kv_w), row), pl.BlockSpec((tm, kv_w), row)],
        out_shape=[jax.ShapeDtypeStruct((t, d), BF16), jax.ShapeDtypeStruct((t, kv_w), BF16),
                   jax.ShapeDtypeStruct((t, kv_w), BF16)],
        compiler_params=_params(1),
        name="qkv",
    )(x, _raw(gains), _raw(w_qkv), _raw(q_norm), _raw(k_norm), cos, sin)


def _attn_kernel(q_ref, k_ref, v_ref, o_ref, *, tk, unroll, sub_tq):
    s = k_ref.shape[1]
    groups = k_ref.shape[2] // HEAD_DIM
    rows = Q_PER_KV * sub_tq
    ones = jnp.ones((tk, HEAD_DIM), BF16)
    head_cols = lambda hd: slice(hd * HEAD_DIM, (hd + 1) * HEAD_DIM)
    streams = [(g, slice(r * sub_tq, (r + 1) * sub_tq))
               for g in range(groups) for r in range(q_ref.shape[1] // sub_tq)]
    qs = [jnp.concatenate([q_ref[0, rs, head_cols(g * Q_PER_KV + hd)] for hd in range(Q_PER_KV)], axis=0)
          for g, rs in streams]

    def body(j, carry):
        start = pl.multiple_of(j * tk, tk)
        out = []
        for (g, _), q, (m, acc) in zip(streams, qs, carry):
            kc = k_ref[0, pl.ds(start, tk), head_cols(g)]
            vc = jnp.concatenate([v_ref[0, pl.ds(start, tk), head_cols(g)], ones], axis=1)
            sc = lax.dot_general(q, kc, (((1,), (1,)), ((), ())), preferred_element_type=F32)
            m_new = jnp.maximum(m, jnp.max(sc, axis=-1, keepdims=True))
            alpha = jnp.exp2(m - m_new)
            p = jnp.exp2(sc - m_new)
            out.append((m_new, alpha * acc + _dot(p.astype(BF16), vc)))
        return tuple(out)

    init = (jnp.full((rows, 1), -jnp.inf, F32), jnp.zeros((rows, 2 * HEAD_DIM), F32))
    final = lax.fori_loop(0, s // tk, body, (init,) * len(streams), unroll=unroll)
    for (g, rs), (_, acc) in zip(streams, final):
        o = acc[:, :HEAD_DIM] / acc[:, HEAD_DIM:]
        for hd in range(Q_PER_KV):
            o_ref[0, rs, head_cols(g * Q_PER_KV + hd)] = o[hd * sub_tq:(hd + 1) * sub_tq, :].astype(BF16)


ATTN_SUB_TQ = 256
ATTN_TK = 512
ATTN_BODY_TRIPS = 16


def _attn_config(s):
    trips = s // ATTN_TK
    groups = N_KV_HEADS if trips * N_KV_HEADS <= ATTN_BODY_TRIPS else 1
    q_blocks = max(1, ATTN_BODY_TRIPS // (trips * groups))
    unroll = min(trips, ATTN_BODY_TRIPS // (groups * q_blocks))
    return dict(tq=q_blocks * ATTN_SUB_TQ, sub_tq=ATTN_SUB_TQ, tk=ATTN_TK, unroll=unroll, groups=groups)


def _attn_call(q, k, v, *, tq, sub_tq, tk, unroll, groups):
    b, s, d = q.shape
    gw = groups * Q_PER_KV * HEAD_DIM
    q_spec = pl.BlockSpec((1, tq, gw), lambda bi, g, i: (bi, i, g))
    kv_spec = pl.BlockSpec((1, s, groups * HEAD_DIM), lambda bi, g, i: (bi, 0, g), pipeline_mode=pl.Buffered(1))
    return pl.pallas_call(
        functools.partial(_attn_kernel, tk=tk, unroll=unroll, sub_tq=sub_tq),
        grid=(b, N_KV_HEADS // groups, s // tq),
        in_specs=[q_spec, kv_spec, kv_spec],
        out_specs=q_spec,
        out_shape=jax.ShapeDtypeStruct(q.shape, BF16),
        compiler_params=_params(3),
        name="attn",
    )(q, k, v)


def _oproj_kernel(x_ref, a_ref, g_ref, w_ref, o_ref):
    y = _dot(a_ref[...], w_ref[...])
    o_ref[...] = x_ref[...] + _rms(y, g_ref[1:2, :])


def _oproj_call(x, a, gains, w, *, tm=512):
    t, d = x.shape
    row_spec = pl.BlockSpec((tm, d), lambda i: (i, 0))
    return pl.pallas_call(
        _oproj_kernel,
        grid=(t // tm,),
        in_specs=[row_spec, row_spec, _const_spec(gains), _const_spec(w)],
        out_specs=row_spec,
        out_shape=jax.ShapeDtypeStruct(x.shape, x.dtype),
        compiler_params=_params(1),
        name="oproj",
    )(x, a, _raw(gains), _raw(w))


def _dft_cos_sin(n):
    idx = jnp.arange(n, dtype=jnp.int32)
    ang = ((idx[:, None] * idx[None, :]) % n).astype(F32) * (2.0 * math.pi / n)
    return jnp.cos(ang), jnp.sin(ang)


FNET_KRON_MAX_N1 = 16
FNET_A_ROWS = 1024


def _fnet_plan(s):
    n1 = s // DFT_N2
    kron = n1 <= FNET_KRON_MAX_N1
    n2a_per_step = max(1, FNET_A_ROWS // (n1 * SUBLANES)) if kron else 1
    return n1, kron, n2a_per_step


def _fnet_tables(s):
    n1, kron, _ = _fnet_plan(s)
    c_ch, s_ch = _dft_cos_sin(C_GROUP_DIM)
    chan = (jnp.concatenate([c_ch, s_ch], axis=1) * C_GROUP_DIM ** -0.5).astype(BF16)
    c1, s1 = _dft_cos_sin(n1)
    if kron:
        eye = jnp.eye(SUBLANES, dtype=F32)
        c1, s1 = jnp.kron(c1, eye), jnp.kron(s1, eye)
    m_a = (jnp.block([[c1, -s1], [-s1, -c1]]) * s ** -0.5).astype(BF16)
    c2, s2 = _dft_cos_sin(DFT_N2)
    m_b = jnp.concatenate([c2, s2], axis=1).astype(BF16)
    k1 = jnp.arange(n1, dtype=jnp.int32)
    n2 = jnp.arange(DFT_N2, dtype=jnp.int32)
    ang = ((k1[:, None] * n2[None, :]) % s).astype(F32) * (2.0 * math.pi / s)
    n2a = DFT_N2 // SUBLANES
    if kron:
        lanes = lambda t: jnp.broadcast_to(t.reshape(n1, n2a, SUBLANES, 1), (n1, n2a, SUBLANES, 128))
    else:
        lanes = lambda t: jnp.broadcast_to(t.T.reshape(n2a, SUBLANES, n1, 1), (n2a, SUBLANES, n1, 128))
    return chan, m_a, m_b, lanes(jnp.cos(ang)), lanes(jnp.sin(ang))


def _fnet_a_kernel(x_ref, g_ref, chan_ref, m_ref, tc_ref, ts_ref, ar_ref, ai_ref, *scratch, kron):
    n1, steps = x_ref.shape[0], x_ref.shape[1]
    rows = n1 * SUBLANES
    panels = D_MODEL // 128
    widen = lambda t: jnp.concatenate([t] * panels, axis=1)
    for a_i in range(steps):
        x = x_ref[:, a_i].reshape(rows, D_MODEL)
        h = _rms(x, g_ref[0:1, :]).astype(BF16)
        zc, zs = [], []
        for g in range(C_GROUPS):
            z = _dot(h[:, g * C_GROUP_DIM:(g + 1) * C_GROUP_DIM], chan_ref[...])
            zc.append(z[:, :C_GROUP_DIM])
            zs.append(z[:, C_GROUP_DIM:])
        if kron:
            z = jnp.concatenate([jnp.concatenate(zc, axis=1), jnp.concatenate(zs, axis=1)], axis=0).astype(BF16)
            a = _dot(m_ref[...], z)
            a_re, a_im = a[:rows, :], a[rows:, :]
            tc = widen(tc_ref[:, a_i].reshape(rows, 128))
            ts = widen(ts_ref[:, a_i].reshape(rows, 128))
            ar_ref[:, a_i] = (a_re * tc + a_im * ts).reshape(n1, SUBLANES, D_MODEL)
            ai_ref[:, a_i] = (a_im * tc - a_re * ts).reshape(n1, SUBLANES, D_MODEL)
        else:
            z_scr, a_scr = scratch
            for c in range(panels):
                z_scr[c] = zc[c]
                z_scr[panels + c] = zs[c]
            for n2b in range(SUBLANES):
                take = lambda p: z_scr[p, pl.ds(n2b, n1, stride=SUBLANES), :]
                z = jnp.concatenate([jnp.concatenate([take(c) for c in range(panels)], axis=1),
                                     jnp.concatenate([take(panels + c) for c in range(panels)], axis=1)],
                                    axis=0).astype(BF16)
                a = _dot(m_ref[...], z)
                a_re, a_im = a[:n1, :], a[n1:, :]
                tc = widen(tc_ref[a_i, n2b])
                ts = widen(ts_ref[a_i, n2b])
                o_re = a_re * tc + a_im * ts
                o_im = a_im * tc - a_re * ts
                for c in range(panels):
                    a_scr[c, pl.ds(n2b, n1, stride=SUBLANES), :] = o_re[:, c * 128:(c + 1) * 128]
                    a_scr[panels + c, pl.ds(n2b, n1, stride=SUBLANES), :] = o_im[:, c * 128:(c + 1) * 128]
            gather = lambda base: jnp.concatenate([a_scr[base + c] for c in range(panels)], axis=1)
            ar_ref[:, a_i] = gather(0).reshape(n1, SUBLANES, D_MODEL)
            ai_ref[:, a_i] = gather(panels).reshape(n1, SUBLANES, D_MODEL)


def _fnet_b_kernel(x_ref, ar_ref, ai_ref, g_ref, cs_ref, w_ref, o_ref, f_scr):
    rows = DFT_N2 * SUBLANES
    for k1b in range(SUBLANES):
        a = jnp.concatenate([ar_ref[k1b].reshape(DFT_N2, D_MODEL), ai_ref[k1b].reshape(DFT_N2, D_MODEL)],
                            axis=0).astype(BF16)
        f = _dot(cs_ref[...], a)
        for c in range(f_scr.shape[0]):
            f_scr[c, pl.ds(k1b, DFT_N2, stride=SUBLANES), :] = f[:, c * 128:(c + 1) * 128]
    f = jnp.concatenate([f_scr[c] for c in range(f_scr.shape[0])], axis=1).astype(BF16)
    y = _dot(f, w_ref[...])
    x = x_ref[...].reshape(rows, D_MODEL)
    o_ref[...] = (x + _rms(y, g_ref[1:2, :])).reshape(o_ref.shape)


def _fnet_call(x, gains, w_out, tables):
    b, s, d = x.shape
    chan, m_a, m_b, tw_c, tw_s = tables
    n1, kron, nb = _fnet_plan(s)
    n2a = DFT_N2 // SUBLANES
    k1a = n1 // SUBLANES
    xa = x.reshape(b, n1, n2a, SUBLANES, d)
    a_spec = pl.BlockSpec((None, n1, nb, SUBLANES, d), lambda bi, j: (bi, 0, j, 0, 0))
    if kron:
        tw_spec = pl.BlockSpec((n1, nb, SUBLANES, 128), lambda bi, j: (0, j, 0, 0))
        scratch = []
    else:
        tw_spec = pl.BlockSpec((nb, SUBLANES, n1, 128), lambda bi, j: (j, 0, 0, 0))
        scratch = [pltpu.VMEM((2 * d // 128, n1 * SUBLANES, 128), F32)] * 2
    a_re, a_im = pl.pallas_call(
        functools.partial(_fnet_a_kernel, kron=kron),
        grid=(b, n2a // nb),
        in_specs=[a_spec, _const_spec(gains), _const_spec(chan), _const_spec(m_a), tw_spec, tw_spec],
        out_specs=[a_spec, a_spec],
        out_shape=[jax.ShapeDtypeStruct(xa.shape, F32)] * 2,
        scratch_shapes=scratch,
        compiler_params=_params(2),
        name="fnet_a",
    )(xa, _raw(gains), chan, m_a, tw_c, tw_s)
    xb = x.reshape(b, DFT_N2, k1a, SUBLANES, d)
    in_spec = pl.BlockSpec((None, SUBLANES, n2a, SUBLANES, d), lambda bi, j: (bi, j, 0, 0, 0))
    x_spec = pl.BlockSpec((None, DFT_N2, None, SUBLANES, d), lambda bi, j: (bi, 0, j, 0, 0))
    out = pl.pallas_call(
        _fnet_b_kernel,
        grid=(b, k1a),
        in_specs=[x_spec, in_spec, in_spec, _const_spec(gains), _const_spec(m_b), _const_spec(w_out)],
        out_specs=x_spec,
        out_shape=jax.ShapeDtypeStruct(xb.shape, F32),
        scratch_shapes=[pltpu.VMEM((d // 128, DFT_N2 * SUBLANES, 128), F32)],
        compiler_params=_params(2),
        name="fnet_b",
    )(xb, a_re, a_im, _raw(gains), m_b, _raw(w_out))
    return out.reshape(b, s, d)


def _trunk(x, p):
    b, s, d = x.shape
    t = b * s
    for i in range(DEPTH):
        m, j = i % N_MIXERS, i // N_MIXERS
        g = _Layer(p["norm_gains"], i)
        mixer = lambda name: _Layer(p[name], j)
        if m == 0:
            x = _gmlp_call(x.reshape(t, d), g, mixer("a_w_in"), mixer("a_v_norm"), mixer("a_w_s"),
                           mixer("a_b_s_t"), mixer("a_w_out")).reshape(b, s, d)
        elif m == 1:
            cos, sin = _rope_tables(s)
            q, k, v = _qkv_call(x.reshape(t, d), g, mixer("b_w_qkv"), mixer("b_q_norm"), mixer("b_k_norm"),
                                cos, sin)
            kv_w = N_KV_HEADS * HEAD_DIM
            a = _attn_call(q.reshape(b, s, d), k.reshape(b, s, kv_w), v.reshape(b, s, kv_w), **_attn_config(s))
            x = _oproj_call(x.reshape(t, d), a.reshape(t, d), g, mixer("b_w_o")).reshape(b, s, d)
        else:
            x = _fnet_call(x, g, mixer("c_w_out"), _fnet_tables(s))
        x = _mlp_call(x.reshape(t, d), g, _Layer(p["mlp_w_in"], i), _Layer(p["mlp_w_out"], i)).reshape(b, s, d)
    return x


def kernel(x_prompt, x_sample, norm_gains, a_w_in, a_v_norm, a_w_s, a_b_s, a_w_out,
           b_w_qkv, b_q_norm, b_k_norm, b_w_o, c_w_out, mlp_w_in, mlp_w_out):
    b_w_qkv, b_q_norm, b_k_norm = _permute_qk_heads(b_w_qkv, b_q_norm, b_k_norm)
    p = {
        "norm_gains": norm_gains,
        "a_w_in": a_w_in.astype(BF16),
        "a_v_norm": a_v_norm[:, None, :],
        "a_w_s": a_w_s.astype(BF16),
        "a_b_s_t": jnp.swapaxes(a_b_s, 1, 2),
        "a_w_out": a_w_out.astype(BF16),
        "b_w_qkv": b_w_qkv.astype(BF16),
        "b_q_norm": b_q_norm[:, None, :],
        "b_k_norm": b_k_norm[:, None, :],
        "b_w_o": b_w_o.astype(BF16),
        "c_w_out": c_w_out.astype(BF16),
        "mlp_w_in": mlp_w_in.astype(BF16),
        "mlp_w_out": mlp_w_out.astype(BF16),
    }
    return (_trunk(x_prompt, p), _trunk(x_sample, p))
```

```python
import functools
import math
from typing import NamedTuple

import jax
import jax.numpy as jnp
from jax import lax
from jax.experimental import pallas as pl
from jax.experimental.pallas import tpu as pltpu

D_MODEL = 1024
DEPTH = 4
N_MIXERS = 3
GRID_W = 64
NORM_EPS = 1e-6
CHUNK = 128
A_WIDTH = 3 * D_MODEL
A_GROUPS = 8
A_GROUP_DIM = A_WIDTH // A_GROUPS
HEAD_DIM = 128
N_HEADS = D_MODEL // HEAD_DIM
N_KV_HEADS = 2
Q_PER_KV = N_HEADS // N_KV_HEADS
AXIS_DIM = HEAD_DIM // 2
ROPE_THETA = 10000.0
C_GROUPS = 8
C_GROUP_DIM = D_MODEL // C_GROUPS
D_FF = 4 * D_MODEL

SUBLANES = 8
VMEM_LIMIT_BYTES = 56 * 1024 * 1024

DFT_N2 = 128

BF16 = jnp.bfloat16
F32 = jnp.float32


def _rms(x, g):
    return x * lax.rsqrt(jnp.mean(x * x, axis=-1, keepdims=True) + NORM_EPS) * g


def _gelu(x):
    return 0.5 * x * (1.0 + lax.erf(x * math.sqrt(0.5)))


def _dot(a, b):
    return jnp.dot(a, b, preferred_element_type=F32)


class _Layer(NamedTuple):
    stacked: jax.Array
    index: int


def _const_spec(operand):
    if isinstance(operand, _Layer):
        shape = (None,) + operand.stacked.shape[1:]
        index = (operand.index,) + (0,) * (len(shape) - 1)
    else:
        shape, index = operand.shape, (0,) * operand.ndim
    return pl.BlockSpec(shape, lambda *_: index, pipeline_mode=pl.Buffered(1))


def _raw(operand):
    return operand.stacked if isinstance(operand, _Layer) else operand


def _params(n_axes):
    return pltpu.CompilerParams(dimension_semantics=("parallel",) * n_axes,
                                vmem_limit_bytes=VMEM_LIMIT_BYTES)


def _mlp_kernel(x_ref, g_ref, w1_ref, w2_ref, o_ref, *, ff_chunk):
    x = x_ref[...]
    h = _rms(x, g_ref[2:3, :]).astype(BF16)
    acc = jnp.zeros(x.shape, F32)
    for c in range(D_FF // ff_chunk):
        cols = slice(c * ff_chunk, (c + 1) * ff_chunk)
        a = _dot(h, w1_ref[:, cols])
        a = jnp.square(jnp.maximum(a, 0.0)).astype(BF16)
        acc = acc + _dot(a, w2_ref[cols, :])
    o_ref[...] = x + _rms(acc, g_ref[3:4, :])


def _mlp_call(x, gains, w1, w2, *, tm=512, ff_chunk=1024):
    t, d = x.shape
    row_spec = pl.BlockSpec((tm, d), lambda i: (i, 0))
    return pl.pallas_call(
        functools.partial(_mlp_kernel, ff_chunk=ff_chunk),
        grid=(t // tm,),
        in_specs=[row_spec, _const_spec(gains), _const_spec(w1), _const_spec(w2)],
        out_specs=row_spec,
        out_shape=jax.ShapeDtypeStruct(x.shape, x.dtype),
        compiler_params=_params(1),
        name="mlp",
    )(x, _raw(gains), _raw(w1), _raw(w2))


def _gmlp_kernel(x_ref, g_ref, w_in_ref, vn_ref, ws_ref, bs_ref, w_out_ref, o_ref, v_scr):
    x = x_ref[...]
    tm = x.shape[0]
    h = _rms(x, g_ref[0:1, :]).astype(BF16)
    pair = 2 * A_GROUP_DIM
    ss = jnp.zeros((tm, 1), F32)
    for gp in range(A_GROUPS // 2):
        cols = slice(gp * pair, (gp + 1) * pair)
        vg = _gelu(_dot(h, w_in_ref[:, A_WIDTH + gp * pair:A_WIDTH + (gp + 1) * pair]))
        ss = ss + jnp.sum(vg * vg, axis=-1, keepdims=True)
        v_scr[:, cols] = vg
    r = lax.rsqrt(ss * (1.0 / A_WIDTH) + NORM_EPS)
    acc = jnp.zeros(x.shape, F32)
    for gp in range(A_GROUPS // 2):
        cols = slice(gp * pair, (gp + 1) * pair)
        vn = (v_scr[:, cols] * r * vn_ref[:, cols]).astype(BF16)
        u = _gelu(_dot(h, w_in_ref[:, cols]))
        gated = []
        for c in range(tm // CHUNK):
            rows = slice(c * CHUNK, (c + 1) * CHUNK)
            halves = []
            for k in range(2):
                g = 2 * gp + k
                gcols = slice(k * A_GROUP_DIM, (k + 1) * A_GROUP_DIM)
                sv = _dot(ws_ref[g], vn[rows, gcols]) + bs_ref[:, g:g + 1]
                halves.append((u[rows, gcols] * sv).astype(BF16))
            gated.append(jnp.concatenate(halves, axis=1))
        acc = acc + _dot(jnp.concatenate(gated, axis=0), w_out_ref[cols, :])
    o_ref[...] = x + _rms(acc, g_ref[1:2, :])


def _gmlp_call(x, gains, w_in, v_norm, w_s, b_s_t, w_out, *, tm=512):
    t, d = x.shape
    row_spec = pl.BlockSpec((tm, d), lambda i: (i, 0))
    return pl.pallas_call(
        _gmlp_kernel,
        grid=(t // tm,),
        in_specs=[row_spec, _const_spec(gains), _const_spec(w_in), _const_spec(v_norm),
                  _const_spec(w_s), _const_spec(b_s_t), _const_spec(w_out)],
        out_specs=row_spec,
        out_shape=jax.ShapeDtypeStruct(x.shape, x.dtype),
        scratch_shapes=[pltpu.VMEM((tm, A_WIDTH), F32)],
        compiler_params=_params(1),
        name="gmlp",
    )(x, _raw(gains), _raw(w_in), _raw(v_norm), _raw(w_s), _raw(b_s_t), _raw(w_out))


_QUARTER = AXIS_DIM // 2
_HEAD_PERM = tuple(list(range(0, _QUARTER)) + list(range(2 * _QUARTER, 3 * _QUARTER))
                   + list(range(_QUARTER, 2 * _QUARTER)) + list(range(3 * _QUARTER, 4 * _QUARTER)))


def _permute_qk_heads(w_qkv, q_norm, k_norm):
    n_rot = (N_HEADS + N_KV_HEADS) * HEAD_DIM
    perm = jnp.asarray(_HEAD_PERM, jnp.int32)
    cols = (jnp.arange(n_rot, dtype=jnp.int32).reshape(-1, HEAD_DIM)[:, perm]).reshape(-1)
    cols = jnp.concatenate([cols, jnp.arange(n_rot, w_qkv.shape[-1], dtype=jnp.int32)])
    return jnp.take(w_qkv, cols, axis=-1), jnp.take(q_norm, perm, axis=-1), jnp.take(k_norm, perm, axis=-1)


def _rope_tables(s):
    pos = jnp.arange(s, dtype=jnp.int32)
    row_pos = (pos // GRID_W).astype(F32)
    col_pos = (pos % GRID_W).astype(F32)
    inv_freq = ROPE_THETA ** (-jnp.arange(0, AXIS_DIM, 2, dtype=F32) / AXIS_DIM)
    ang_row = row_pos[:, None] * inv_freq[None, :]
    ang_col = col_pos[:, None] * inv_freq[None, :]
    cos = jnp.concatenate([jnp.cos(ang_row), jnp.cos(ang_col)] * 2, axis=-1)
    sin = jnp.concatenate([-jnp.sin(ang_row), -jnp.sin(ang_col), jnp.sin(ang_row), jnp.sin(ang_col)], axis=-1)
    return cos, sin


def _qkv_kernel(x_ref, g_ref, w_ref, qn_ref, kn_ref, cos_ref, sin_ref, q_ref, k_ref, v_ref):
    h = _rms(x_ref[...], g_ref[0:1, :]).astype(BF16)
    cos = cos_ref[...]
    sin = sin_ref[...]
    scale = HEAD_DIM ** -0.5 * math.log2(math.e)

    pair = 2 * HEAD_DIM
    blk = lambda axis: lax.broadcasted_iota(jnp.int32, (pair, pair), axis) // HEAD_DIM
    head_ones = (blk(0) == blk(1)).astype(BF16)

    def norm_rope(y, gain):
        sq = y * y
        hi = sq.astype(BF16)
        lo = (sq - hi.astype(F32)).astype(BF16)
        ms = (_dot(hi, head_ones) + _dot(lo, head_ones)) * (1.0 / HEAD_DIM)
        heads = []
        for k in range(2):
            cols = slice(k * HEAD_DIM, (k + 1) * HEAD_DIM)
            t = y[:, cols] * lax.rsqrt(ms[:, cols] + NORM_EPS) * gain
            heads.append(t * cos + pltpu.roll(t, HEAD_DIM // 2, 1) * sin)
        return heads

    n_q_tiles = N_HEADS // 2
    n_k_tiles = N_KV_HEADS // 2
    wide = 2 * pair
    for blk_i in range(w_ref.shape[1] // wide):
        yy = _dot(h, w_ref[:, blk_i * wide:(blk_i + 1) * wide])
        for half in range(2):
            tile = 2 * blk_i + half
            cols = slice(tile * pair, (tile + 1) * pair)
            y = yy[:, half * pair:(half + 1) * pair]
            if tile < n_q_tiles:
                q_ref[:, cols] = (jnp.concatenate(norm_rope(y, qn_ref[...]), axis=1) * scale).astype(BF16)
            elif tile < n_q_tiles + n_k_tiles:
                kcols = slice((tile - n_q_tiles) * pair, (tile - n_q_tiles + 1) * pair)
                k_ref[:, kcols] = jnp.concatenate(norm_rope(y, kn_ref[...]), axis=1).astype(BF16)
            else:
                vcols = slice((tile - n_q_tiles - n_k_tiles) * pair, (tile - n_q_tiles - n_k_tiles + 1) * pair)
                v_ref[:, vcols] = y.astype(BF16)


def _qkv_call(x, gains, w_qkv, q_norm, k_norm, cos, sin, *, tm=512):
    t, d = x.shape
    s = cos.shape[0]
    blocks_per_seq = s // tm
    row = lambda i: (i, 0)
    pos = lambda i: (i % blocks_per_seq, 0)
    kv_w = N_KV_HEADS * HEAD_DIM
    return pl.pallas_call(
        _qkv_kernel,
        grid=(t // tm,),
        in_specs=[pl.BlockSpec((tm, d), row), _const_spec(gains), _const_spec(w_qkv),
                  _const_spec(q_norm), _const_spec(k_norm),
                  pl.BlockSpec((tm, HEAD_DIM), pos), pl.BlockSpec((tm, HEAD_DIM), pos)],
        out_specs=[pl.BlockSpec((tm, d), row), pl.BlockSpec((tm, kv_w), row), pl.BlockSpec((tm, kv_w), row)],
        out_shape=[jax.ShapeDtypeStruct((t, d), BF16), jax.ShapeDtypeStruct((t, kv_w), BF16),
                   jax.ShapeDtypeStruct((t, kv_w), BF16)],
        compiler_params=_params(1),
        name="qkv",
    )(x, _raw(gains), _raw(w_qkv), _raw(q_norm), _raw(k_norm), cos, sin)


def _attn_kernel(x_ref, q_ref, k_ref, v_ref, g_ref, wo_ref, o_ref, *, tk, unroll, sub_tq, phase_streams):
    s = k_ref.shape[1]
    groups = N_KV_HEADS
    rows = Q_PER_KV * sub_tq
    ones = jnp.ones((tk, HEAD_DIM), BF16)
    head_cols = lambda hd: slice(hd * HEAD_DIM, (hd + 1) * HEAD_DIM)
    streams = [(g, slice(r * sub_tq, (r + 1) * sub_tq))
               for g in range(groups) for r in range(q_ref.shape[1] // sub_tq)]
    qs = [jnp.concatenate([q_ref[0, rs, head_cols(g * Q_PER_KV + hd)] for hd in range(Q_PER_KV)], axis=0)
          for g, rs in streams]

    def run(phase):
        def body(j, carry):
            start = pl.multiple_of(j * tk, tk)
            out = []
            for i, (m, acc) in zip(phase, carry):
                g = streams[i][0]
                kc = k_ref[0, pl.ds(start, tk), head_cols(g)]
                vc = jnp.concatenate([v_ref[0, pl.ds(start, tk), head_cols(g)], ones], axis=1)
                sc = lax.dot_general(qs[i], kc, (((1,), (1,)), ((), ())), preferred_element_type=F32)
                m_new = jnp.maximum(m, jnp.max(sc, axis=-1, keepdims=True))
                alpha = jnp.exp2(m - m_new)
                p = jnp.exp2(sc - m_new)
                out.append((m_new, alpha * acc + _dot(p.astype(BF16), vc)))
            return tuple(out)

        init = (jnp.full((rows, 1), -jnp.inf, F32), jnp.zeros((rows, 2 * HEAD_DIM), F32))
        return lax.fori_loop(0, s // tk, body, (init,) * len(phase), unroll=unroll)

    final = []
    for first in range(0, len(streams), phase_streams):
        final.extend(run(range(first, first + phase_streams)))
    heads = {}
    for (g, rs), (_, acc) in zip(streams, final):
        o = (acc[:, :HEAD_DIM] / acc[:, HEAD_DIM:]).astype(BF16)
        for hd in range(Q_PER_KV):
            heads[(rs.start, g * Q_PER_KV + hd)] = o[hd * sub_tq:(hd + 1) * sub_tq, :]
    for r0 in sorted({r for r, _ in heads}):
        a = jnp.concatenate([heads[(r0, hd)] for hd in range(N_HEADS)], axis=1)
        y = _dot(a, wo_ref[...])
        rows_r = slice(r0, r0 + sub_tq)
        o_ref[0, rows_r, :] = x_ref[0, rows_r, :] + _rms(y, g_ref[1:2, :])


ATTN_SUB_TQ = 256
ATTN_TK = 512
ATTN_BODY_TRIPS = 16


def _attn_config(s):
    trips = s // ATTN_TK
    phase_streams = max(1, ATTN_BODY_TRIPS // trips)
    q_blocks = max(1, phase_streams // N_KV_HEADS)
    phase_streams = min(phase_streams, N_KV_HEADS * q_blocks)
    return dict(tq=q_blocks * ATTN_SUB_TQ, sub_tq=ATTN_SUB_TQ, tk=ATTN_TK, phase_streams=phase_streams,
                unroll=min(trips, ATTN_BODY_TRIPS // phase_streams))


def _attn_call(x, q, k, v, gains, w_o, *, tq, sub_tq, tk, unroll, phase_streams):
    b, s, d = q.shape
    row_spec = pl.BlockSpec((1, tq, d), lambda bi, i: (bi, i, 0))
    kv_spec = pl.BlockSpec((1, s, N_KV_HEADS * HEAD_DIM), lambda bi, i: (bi, 0, 0), pipeline_mode=pl.Buffered(1))
    return pl.pallas_call(
        functools.partial(_attn_kernel, tk=tk, unroll=unroll, sub_tq=sub_tq, phase_streams=phase_streams),
        grid=(b, s // tq),
        in_specs=[row_spec, row_spec, kv_spec, kv_spec, _const_spec(gains), _const_spec(w_o)],
        out_specs=row_spec,
        out_shape=jax.ShapeDtypeStruct(x.shape, x.dtype),
        compiler_params=_params(2),
        name="attn",
    )(x, q, k, v, _raw(gains), _raw(w_o))


def _dft_cos_sin(n):
    idx = jnp.arange(n, dtype=jnp.int32)
    ang = ((idx[:, None] * idx[None, :]) % n).astype(F32) * (2.0 * math.pi / n)
    return jnp.cos(ang), jnp.sin(ang)


FNET_KRON_MAX_N1 = 16
FNET_A_ROWS = 1024


def _fnet_plan(s):
    n1 = s // DFT_N2
    kron = n1 <= FNET_KRON_MAX_N1
    n2a_per_step = max(1, FNET_A_ROWS // (n1 * SUBLANES)) if kron else 1
    return n1, kron, n2a_per_step


def _fnet_tables(s):
    n1, kron, _ = _fnet_plan(s)
    c_ch, s_ch = _dft_cos_sin(C_GROUP_DIM)
    chan = (jnp.concatenate([c_ch, s_ch], axis=1) * C_GROUP_DIM ** -0.5).astype(BF16)
    c1, s1 = _dft_cos_sin(n1)
    if kron:
        eye = jnp.eye(SUBLANES, dtype=F32)
        c1, s1 = jnp.kron(c1, eye), jnp.kron(s1, eye)
    m_a = (jnp.block([[c1, -s1], [-s1, -c1]]) * s ** -0.5).astype(BF16)
    c2, s2 = _dft_cos_sin(DFT_N2)
    m_b = jnp.concatenate([c2, s2], axis=1).astype(BF16)
    k1 = jnp.arange(n1, dtype=jnp.int32)
    n2 = jnp.arange(DFT_N2, dtype=jnp.int32)
    ang = ((k1[:, None] * n2[None, :]) % s).astype(F32) * (2.0 * math.pi / s)
    n2a = DFT_N2 // SUBLANES
    if kron:
        lanes = lambda t: jnp.broadcast_to(t.reshape(n1, n2a, SUBLANES, 1), (n1, n2a, SUBLANES, 128))
    else:
        lanes = lambda t: jnp.tile(t.reshape(n1, n2a, SUBLANES).transpose(1, 0, 2), (1, 1, 128 // SUBLANES))
    return chan, m_a, m_b, lanes(jnp.cos(ang)), lanes(jnp.sin(ang))


def _fnet_a_kernel(x_ref, g_ref, chan_ref, m_ref, tc_ref, ts_ref, ar_ref, ai_ref, *scratch, kron):
    n1, steps = x_ref.shape[0], x_ref.shape[1]
    rows = n1 * SUBLANES
    panels = D_MODEL // 128
    widen = lambda t: jnp.concatenate([t] * panels, axis=1)
    for a_i in range(steps):
        x = x_ref[:, a_i].reshape(rows, D_MODEL)
        h = _rms(x, g_ref[0:1, :]).astype(BF16)
        zc, zs = [], []
        for g in range(C_GROUPS):
            z = _dot(h[:, g * C_GROUP_DIM:(g + 1) * C_GROUP_DIM], chan_ref[...])
            zc.append(z[:, :C_GROUP_DIM])
            zs.append(z[:, C_GROUP_DIM:])
        if kron:
            z = jnp.concatenate([jnp.concatenate(zc, axis=1), jnp.concatenate(zs, axis=1)], axis=0).astype(BF16)
            a = _dot(m_ref[...], z)
            a_re, a_im = a[:rows, :], a[rows:, :]
            tc = widen(tc_ref[:, a_i].reshape(rows, 128))
            ts = widen(ts_ref[:, a_i].reshape(rows, 128))
            ar_ref[:, a_i] = (a_re * tc + a_im * ts).reshape(n1, SUBLANES, D_MODEL)
            ai_ref[:, a_i] = (a_im * tc - a_re * ts).reshape(n1, SUBLANES, D_MODEL)
        else:
            z_scr, a_scr = scratch
            for c in range(panels):
                z_scr[c] = zc[c]
                z_scr[panels + c] = zs[c]
            for n2b in range(SUBLANES):
                take = lambda p: z_scr[p, pl.ds(n2b, n1, stride=SUBLANES), :]
                z = jnp.concatenate([jnp.concatenate([take(c) for c in range(panels)], axis=1),
                                     jnp.concatenate([take(panels + c) for c in range(panels)], axis=1)],
                                    axis=0).astype(BF16)
                a = _dot(m_ref[...], z)
                a_re, a_im = a[:n1, :], a[n1:, :]
                one_lane = lambda ref: widen(jnp.broadcast_to(ref[a_i][:, n2b:n2b + 1], (n1, 128)))
                tc = one_lane(tc_ref)
                ts = one_lane(ts_ref)
                o_re = a_re * tc + a_im * ts
                o_im = a_im * tc - a_re * ts
                for c in range(panels):
                    a_scr[c, pl.ds(n2b, n1, stride=SUBLANES), :] = o_re[:, c * 128:(c + 1) * 128]
                    a_scr[panels + c, pl.ds(n2b, n1, stride=SUBLANES), :] = o_im[:, c * 128:(c + 1) * 128]
            gather = lambda base: jnp.concatenate([a_scr[base + c] for c in range(panels)], axis=1)
            ar_ref[:, a_i] = gather(0).reshape(n1, SUBLANES, D_MODEL)
            ai_ref[:, a_i] = gather(panels).reshape(n1, SUBLANES, D_MODEL)


def _fnet_b_kernel(x_ref, ar_ref, ai_ref, g_ref, cs_ref, w_ref, o_ref, f_scr):
    rows = DFT_N2 * SUBLANES
    for k1b in range(SUBLANES):
        a = jnp.concatenate([ar_ref[k1b].reshape(DFT_N2, D_MODEL), ai_ref[k1b].reshape(DFT_N2, D_MODEL)],
                            axis=0).astype(BF16)
        f = _dot(cs_ref[...], a)
        for c in range(f_scr.shape[0]):
            f_scr[c, pl.ds(k1b, DFT_N2, stride=SUBLANES), :] = f[:, c * 128:(c + 1) * 128]
    f = jnp.concatenate([f_scr[c] for c in range(f_scr.shape[0])], axis=1).astype(BF16)
    y = _dot(f, w_ref[...])
    x = x_ref[...].reshape(rows, D_MODEL)
    o_ref[...] = (x + _rms(y, g_ref[1:2, :])).reshape(o_ref.shape)


def _fnet_call(x, gains, w_out, tables):
    b, s, d = x.shape
    chan, m_a, m_b, tw_c, tw_s = tables
    n1, kron, nb = _fnet_plan(s)
    n2a = DFT_N2 // SUBLANES
    k1a = n1 // SUBLANES
    xa = x.reshape(b, n1, n2a, SUBLANES, d)
    a_spec = pl.BlockSpec((None, n1, nb, SUBLANES, d), lambda bi, j: (bi, 0, j, 0, 0))
    if kron:
        tw_spec = pl.BlockSpec((n1, nb, SUBLANES, 128), lambda bi, j: (0, j, 0, 0))
        scratch = []
    else:
        tw_spec = pl.BlockSpec((nb, n1, 128), lambda bi, j: (j, 0, 0))
        scratch = [pltpu.VMEM((2 * d // 128, n1 * SUBLANES, 128), F32)] * 2
    a_re, a_im = pl.pallas_call(
        functools.partial(_fnet_a_kernel, kron=kron),
        grid=(b, n2a // nb),
        in_specs=[a_spec, _const_spec(gains), _const_spec(chan), _const_spec(m_a), tw_spec, tw_spec],
        out_specs=[a_spec, a_spec],
        out_shape=[jax.ShapeDtypeStruct(xa.shape, F32)] * 2,
        scratch_shapes=scratch,
        compiler_params=_params(2),
        name="fnet_a",
    )(xa, _raw(gains), chan, m_a, tw_c, tw_s)
    xb = x.reshape(b, DFT_N2, k1a, SUBLANES, d)
    in_spec = pl.BlockSpec((None, SUBLANES, n2a, SUBLANES, d), lambda bi, j: (bi, j, 0, 0, 0))
    x_spec = pl.BlockSpec((None, DFT_N2, None, SUBLANES, d), lambda bi, j: (bi, 0, j, 0, 0))
    out = pl.pallas_call(
        _fnet_b_kernel,
        grid=(b, k1a),
        in_specs=[x_spec, in_spec, in_spec, _const_spec(gains), _const_spec(m_b), _const_spec(w_out)],
        out_specs=x_spec,
        out_shape=jax.ShapeDtypeStruct(xb.shape, F32),
        scratch_shapes=[pltpu.VMEM((d // 128, DFT_N2 * SUBLANES, 128), F32)],
        compiler_params=_params(2),
        name="fnet_b",
    )(xb, a_re, a_im, _raw(gains), m_b, _raw(w_out))
    return out.reshape(b, s, d)


def _trunk(x, p):
    b, s, d = x.shape
    t = b * s
    for i in range(DEPTH):
        m, j = i % N_MIXERS, i // N_MIXERS
        g = _Layer(p["norm_gains"], i)
        mixer = lambda name: _Layer(p[name], j)
        if m == 0:
            x = _gmlp_call(x.reshape(t, d), g, mixer("a_w_in"), mixer("a_v_norm"), mixer("a_w_s"),
                           mixer("a_b_s_t"), mixer("a_w_out")).reshape(b, s, d)
        elif m == 1:
            cos, sin = _rope_tables(s)
            q, k, v = _qkv_call(x.reshape(t, d), g, mixer("b_w_qkv"), mixer("b_q_norm"), mixer("b_k_norm"),
                                cos, sin)
            kv_w = N_KV_HEADS * HEAD_DIM
            x = _attn_call(x, q.reshape(b, s, d), k.reshape(b, s, kv_w), v.reshape(b, s, kv_w), g, mixer("b_w_o"),
                           **_attn_config(s))
        else:
            x = _fnet_call(x, g, mixer("c_w_out"), _fnet_tables(s))
        x = _mlp_call(x.reshape(t, d), g, _Layer(p["mlp_w_in"], i), _Layer(p["mlp_w_out"], i)).reshape(b, s, d)
    return x


def kernel(x_prompt, x_sample, norm_gains, a_w_in, a_v_norm, a_w_s, a_b_s, a_w_out,
           b_w_qkv, b_q_norm, b_k_norm, b_w_o, c_w_out, mlp_w_in, mlp_w_out):
    b_w_qkv, b_q_norm, b_k_norm = _permute_qk_heads(b_w_qkv, b_q_norm, b_k_norm)
    p = {
        "norm_gains": norm_gains,
        "a_w_in": a_w_in.astype(BF16),
        "a_v_norm": a_v_norm[:, None, :],
        "a_w_s": a_w_s.astype(BF16),
        "a_b_s_t": jnp.swapaxes(a_b_s, 1, 2),
        "a_w_out": a_w_out.astype(BF16),
        "b_w_qkv": b_w_qkv.astype(BF16),
        "b_q_norm": b_q_norm[:, None, :],
        "b_k_norm": b_k_norm[:, None, :],
        "b_w_o": b_w_o.astype(BF16),
        "c_w_out": c_w_out.astype(BF16),
        "mlp_w_in": mlp_w_in.astype(BF16),
        "mlp_w_out": mlp_w_out.astype(BF16),
    }
    return (_trunk(x_prompt, p), _trunk(x_sample, p))
```

```python
import functools
import math
from typing import NamedTuple

import jax
import jax.numpy as jnp
import numpy as np
from jax import lax
from jax.experimental import pallas as pl
from jax.experimental.pallas import tpu as pltpu

D_MODEL = 1024
DEPTH = 4
N_MIXERS = 3
GRID_W = 64
NORM_EPS = 1e-6
CHUNK = 128
A_WIDTH = 3 * D_MODEL
A_GROUPS = 8
A_GROUP_DIM = A_WIDTH // A_GROUPS
HEAD_DIM = 128
N_HEADS = D_MODEL // HEAD_DIM
N_KV_HEADS = 2
Q_PER_KV = N_HEADS // N_KV_HEADS
AXIS_DIM = HEAD_DIM // 2
ROPE_THETA = 10000.0
C_GROUPS = 8
C_GROUP_DIM = D_MODEL // C_GROUPS
D_FF = 4 * D_MODEL

SUBLANES = 8
VMEM_LIMIT_BYTES = 56 * 1024 * 1024

DFT_N2 = 128

BF16 = jnp.bfloat16
F32 = jnp.float32


def _rms(x, g):
    return x * lax.rsqrt(jnp.mean(x * x, axis=-1, keepdims=True) + NORM_EPS) * g


def _gelu(x):
    return 0.5 * x * (1.0 + lax.erf(x * math.sqrt(0.5)))


def _dot(a, b):
    return jnp.dot(a, b, preferred_element_type=F32)


class _Layer(NamedTuple):
    stacked: jax.Array
    index: int


def _const_spec(operand):
    if isinstance(operand, _Layer):
        shape = (None,) + operand.stacked.shape[1:]
        index = (operand.index,) + (0,) * (len(shape) - 1)
    else:
        shape, index = operand.shape, (0,) * operand.ndim
    return pl.BlockSpec(shape, lambda *_: index, pipeline_mode=pl.Buffered(1))


def _raw(operand):
    return operand.stacked if isinstance(operand, _Layer) else operand


def _params(n_axes):
    return pltpu.CompilerParams(dimension_semantics=("parallel",) * n_axes,
                                vmem_limit_bytes=VMEM_LIMIT_BYTES)


def _mlp_kernel(x_ref, g_ref, w1_ref, w2_ref, o_ref, *, ff_chunk):
    x = x_ref[...]
    h = _rms(x, g_ref[2:3, :]).astype(BF16)
    acc = jnp.zeros(x.shape, F32)
    for c in range(D_FF // ff_chunk):
        cols = slice(c * ff_chunk, (c + 1) * ff_chunk)
        a = _dot(h, w1_ref[:, cols])
        a = jnp.square(jnp.maximum(a, 0.0)).astype(BF16)
        acc = acc + _dot(a, w2_ref[cols, :])
    o_ref[...] = x + _rms(acc, g_ref[3:4, :])


def _mlp_call(x, gains, w1, w2, *, tm=512, ff_chunk=1024):
    t, d = x.shape
    row_spec = pl.BlockSpec((tm, d), lambda i: (i, 0))
    return pl.pallas_call(
        functools.partial(_mlp_kernel, ff_chunk=ff_chunk),
        grid=(t // tm,),
        in_specs=[row_spec, _const_spec(gains), _const_spec(w1), _const_spec(w2)],
        out_specs=row_spec,
        out_shape=jax.ShapeDtypeStruct(x.shape, x.dtype),
        compiler_params=_params(1),
        name="mlp",
    )(x, _raw(gains), _raw(w1), _raw(w2))


def _gmlp_kernel(x_ref, g_ref, w_in_ref, vn_ref, ws_ref, bs_ref, w_out_ref, o_ref, v_scr):
    x = x_ref[...]
    tm = x.shape[0]
    h = _rms(x, g_ref[0:1, :]).astype(BF16)
    pair = 2 * A_GROUP_DIM
    ss = jnp.zeros((tm, 1), F32)
    for gp in range(A_GROUPS // 2):
        cols = slice(gp * pair, (gp + 1) * pair)
        vg = _gelu(_dot(h, w_in_ref[:, A_WIDTH + gp * pair:A_WIDTH + (gp + 1) * pair]))
        ss = ss + jnp.sum(vg * vg, axis=-1, keepdims=True)
        v_scr[:, cols] = vg
    r = lax.rsqrt(ss * (1.0 / A_WIDTH) + NORM_EPS)
    acc = jnp.zeros(x.shape, F32)
    for gp in range(A_GROUPS // 2):
        cols = slice(gp * pair, (gp + 1) * pair)
        vn = (v_scr[:, cols] * r * vn_ref[:, cols]).astype(BF16)
        u = _gelu(_dot(h, w_in_ref[:, cols]))
        gated = []
        for c in range(tm // CHUNK):
            rows = slice(c * CHUNK, (c + 1) * CHUNK)
            halves = []
            for k in range(2):
                g = 2 * gp + k
                gcols = slice(k * A_GROUP_DIM, (k + 1) * A_GROUP_DIM)
                sv = _dot(ws_ref[g], vn[rows, gcols]) + bs_ref[:, g:g + 1]
                halves.append((u[rows, gcols] * sv).astype(BF16))
            gated.append(jnp.concatenate(halves, axis=1))
        acc = acc + _dot(jnp.concatenate(gated, axis=0), w_out_ref[cols, :])
    o_ref[...] = x + _rms(acc, g_ref[1:2, :])


def _gmlp_call(x, gains, w_in, v_norm, w_s, b_s_t, w_out, *, tm=512):
    t, d = x.shape
    row_spec = pl.BlockSpec((tm, d), lambda i: (i, 0))
    return pl.pallas_call(
        _gmlp_kernel,
        grid=(t // tm,),
        in_specs=[row_spec, _const_spec(gains), _const_spec(w_in), _const_spec(v_norm),
                  _const_spec(w_s), _const_spec(b_s_t), _const_spec(w_out)],
        out_specs=row_spec,
        out_shape=jax.ShapeDtypeStruct(x.shape, x.dtype),
        scratch_shapes=[pltpu.VMEM((tm, A_WIDTH), F32)],
        compiler_params=_params(1),
        name="gmlp",
    )(x, _raw(gains), _raw(w_in), _raw(v_norm), _raw(w_s), _raw(b_s_t), _raw(w_out))


_QUARTER = AXIS_DIM // 2
_HEAD_PERM = tuple(list(range(0, _QUARTER)) + list(range(2 * _QUARTER, 3 * _QUARTER))
                   + list(range(_QUARTER, 2 * _QUARTER)) + list(range(3 * _QUARTER, 4 * _QUARTER)))


def _permute_qk_heads(w_qkv, q_norm, k_norm):
    n_rot = (N_HEADS + N_KV_HEADS) * HEAD_DIM
    perm = jnp.asarray(_HEAD_PERM, jnp.int32)
    cols = (jnp.arange(n_rot, dtype=jnp.int32).reshape(-1, HEAD_DIM)[:, perm]).reshape(-1)
    cols = jnp.concatenate([cols, jnp.arange(n_rot, w_qkv.shape[-1], dtype=jnp.int32)])
    return jnp.take(w_qkv, cols, axis=-1), jnp.take(q_norm, perm, axis=-1), jnp.take(k_norm, perm, axis=-1)


def _rope_tables(s):
    rows = s // GRID_W
    inv_freq = ROPE_THETA ** (-jnp.arange(0, AXIS_DIM, 2, dtype=F32) / AXIS_DIM)
    ang_row = jnp.arange(rows, dtype=F32)[:, None] * inv_freq[None, :]
    ang_col = jnp.arange(GRID_W, dtype=F32)[:, None] * inv_freq[None, :]
    shape = (rows, GRID_W, AXIS_DIM // 2)
    by_row = lambda t: jnp.broadcast_to(t[:, None, :], shape)
    by_col = lambda t: jnp.broadcast_to(t[None, :, :], shape)
    cr, sr, cc, sc = by_row(jnp.cos(ang_row)), by_row(jnp.sin(ang_row)), by_col(jnp.cos(ang_col)), by_col(jnp.sin(ang_col))
    cos = jnp.concatenate([cr, cc, cr, cc], axis=-1).reshape(s, HEAD_DIM)
    sin = jnp.concatenate([-sr, -sc, sr, sc], axis=-1).reshape(s, HEAD_DIM)
    return cos, sin


def _qkv_kernel(x_ref, g_ref, w_ref, qn_ref, kn_ref, cos_ref, sin_ref, q_ref, k_ref, v_ref):
    h = _rms(x_ref[...], g_ref[0:1, :]).astype(BF16)
    cos = cos_ref[...]
    sin = sin_ref[...]
    scale = HEAD_DIM ** -0.5 * math.log2(math.e)

    pair = 2 * HEAD_DIM
    blk = lambda axis: lax.broadcasted_iota(jnp.int32, (pair, pair), axis) // HEAD_DIM
    head_ones = (blk(0) == blk(1)).astype(BF16)

    def norm_rope(y, gain):
        sq = y * y
        hi = sq.astype(BF16)
        lo = (sq - hi.astype(F32)).astype(BF16)
        ms = (_dot(hi, head_ones) + _dot(lo, head_ones)) * (1.0 / HEAD_DIM)
        heads = []
        for k in range(2):
            cols = slice(k * HEAD_DIM, (k + 1) * HEAD_DIM)
            t = y[:, cols] * lax.rsqrt(ms[:, cols] + NORM_EPS) * gain
            heads.append(t * cos + pltpu.roll(t, HEAD_DIM // 2, 1) * sin)
        return heads

    n_q_tiles = N_HEADS // 2
    n_k_tiles = N_KV_HEADS // 2
    wide = 2 * pair
    for blk_i in range(w_ref.shape[1] // wide):
        yy = _dot(h, w_ref[:, blk_i * wide:(blk_i + 1) * wide])
        for half in range(2):
            tile = 2 * blk_i + half
            cols = slice(tile * pair, (tile + 1) * pair)
            y = yy[:, half * pair:(half + 1) * pair]
            if tile < n_q_tiles:
                q_ref[:, cols] = (jnp.concatenate(norm_rope(y, qn_ref[...]), axis=1) * scale).astype(BF16)
            elif tile < n_q_tiles + n_k_tiles:
                kcols = slice((tile - n_q_tiles) * pair, (tile - n_q_tiles + 1) * pair)
                k_ref[:, kcols] = jnp.concatenate(norm_rope(y, kn_ref[...]), axis=1).astype(BF16)
            else:
                vcols = slice((tile - n_q_tiles - n_k_tiles) * pair, (tile - n_q_tiles - n_k_tiles + 1) * pair)
                v_ref[:, vcols] = y.astype(BF16)


def _qkv_call(x, gains, w_qkv, q_norm, k_norm, cos, sin, *, tm=512):
    t, d = x.shape
    s = cos.shape[0]
    blocks_per_seq = s // tm
    row = lambda i: (i, 0)
    pos = lambda i: (i % blocks_per_seq, 0)
    kv_w = N_KV_HEADS * HEAD_DIM
    return pl.pallas_call(
        _qkv_kernel,
        grid=(t // tm,),
        in_specs=[pl.BlockSpec((tm, d), row), _const_spec(gains), _const_spec(w_qkv),
                  _const_spec(q_norm), _const_spec(k_norm),
                  pl.BlockSpec((tm, HEAD_DIM), pos), pl.BlockSpec((tm, HEAD_DIM), pos)],
        out_specs=[pl.BlockSpec((tm, d), row), pl.BlockSpec((tm, kv_w), row), pl.BlockSpec((tm, kv_w), row)],
        out_shape=[jax.ShapeDtypeStruct((t, d), BF16), jax.ShapeDtypeStruct((t, kv_w), BF16),
                   jax.ShapeDtypeStruct((t, kv_w), BF16)],
        compiler_params=_params(1),
        name="qkv",
    )(x, _raw(gains), _raw(w_qkv), _raw(q_norm), _raw(k_norm), cos, sin)


def _attn_kernel(x_ref, q_ref, k_ref, v_ref, g_ref, wo_ref, o_ref, *, tk, unroll, sub_tq, phase_streams):
    s = k_ref.shape[1]
    groups = N_KV_HEADS
    rows = Q_PER_KV * sub_tq
    ones = jnp.ones((tk, HEAD_DIM), BF16)
    head_cols = lambda hd: slice(hd * HEAD_DIM, (hd + 1) * HEAD_DIM)
    streams = [(g, slice(r * sub_tq, (r + 1) * sub_tq))
               for g in range(groups) for r in range(q_ref.shape[1] // sub_tq)]
    qs = [jnp.concatenate([q_ref[0, rs, head_cols(g * Q_PER_KV + hd)] for hd in range(Q_PER_KV)], axis=0)
          for g, rs in streams]

    def run(phase):
        def body(j, carry):
            start = pl.multiple_of(j * tk, tk)
            out = []
            for i, (m, acc) in zip(phase, carry):
                g = streams[i][0]
                kc = k_ref[0, pl.ds(start, tk), head_cols(g)]
                vc = jnp.concatenate([v_ref[0, pl.ds(start, tk), head_cols(g)], ones], axis=1)
                sc = lax.dot_general(qs[i], kc, (((1,), (1,)), ((), ())), preferred_element_type=F32)
                m_new = jnp.maximum(m, jnp.max(sc, axis=-1, keepdims=True))
                alpha = jnp.exp2(m - m_new)
                p = jnp.exp2(sc - m_new)
                out.append((m_new, alpha * acc + _dot(p.astype(BF16), vc)))
            return tuple(out)

        init = (jnp.full((rows, 1), -jnp.inf, F32), jnp.zeros((rows, 2 * HEAD_DIM), F32))
        return lax.fori_loop(0, s // tk, body, (init,) * len(phase), unroll=unroll)

    final = []
    for first in range(0, len(streams), phase_streams):
        final.extend(run(range(first, first + phase_streams)))
    heads = {}
    for (g, rs), (_, acc) in zip(streams, final):
        o = (acc[:, :HEAD_DIM] / acc[:, HEAD_DIM:]).astype(BF16)
        for hd in range(Q_PER_KV):
            heads[(rs.start, g * Q_PER_KV + hd)] = o[hd * sub_tq:(hd + 1) * sub_tq, :]
    for r0 in sorted({r for r, _ in heads}):
        a = jnp.concatenate([heads[(r0, hd)] for hd in range(N_HEADS)], axis=1)
        y = _dot(a, wo_ref[...])
        rows_r = slice(r0, r0 + sub_tq)
        o_ref[0, rows_r, :] = x_ref[0, rows_r, :] + _rms(y, g_ref[1:2, :])


ATTN_SUB_TQ = 256
ATTN_TK = 512
ATTN_BODY_TRIPS = 16


def _attn_config(s):
    trips = s // ATTN_TK
    phase_streams = max(1, ATTN_BODY_TRIPS // trips)
    q_blocks = max(1, phase_streams // N_KV_HEADS)
    phase_streams = min(phase_streams, N_KV_HEADS * q_blocks)
    return dict(tq=q_blocks * ATTN_SUB_TQ, sub_tq=ATTN_SUB_TQ, tk=ATTN_TK, phase_streams=phase_streams,
                unroll=min(trips, ATTN_BODY_TRIPS // phase_streams))


def _attn_call(x, q, k, v, gains, w_o, *, tq, sub_tq, tk, unroll, phase_streams):
    b, s, d = q.shape
    row_spec = pl.BlockSpec((1, tq, d), lambda bi, i: (bi, i, 0))
    kv_spec = pl.BlockSpec((1, s, N_KV_HEADS * HEAD_DIM), lambda bi, i: (bi, 0, 0), pipeline_mode=pl.Buffered(1))
    return pl.pallas_call(
        functools.partial(_attn_kernel, tk=tk, unroll=unroll, sub_tq=sub_tq, phase_streams=phase_streams),
        grid=(b, s // tq),
        in_specs=[row_spec, row_spec, kv_spec, kv_spec, _const_spec(gains), _const_spec(w_o)],
        out_specs=row_spec,
        out_shape=jax.ShapeDtypeStruct(x.shape, x.dtype),
        compiler_params=_params(2),
        name="attn",
    )(x, q, k, v, _raw(gains), _raw(w_o))


BF16_SUBLANES = 16
FNET_KRON_MAX_N1 = 16
FNET_A_ROWS = 1024


class _FnetPlan(NamedTuple):
    n1: int
    kron: bool
    tile: int
    a_dtype: object
    groups_per_step: int


def _dft_cos_sin(n):
    idx = np.arange(n)
    ang = (np.outer(idx, idx) % n) * (2.0 * np.pi / n)
    return np.cos(ang), np.sin(ang)


def _fnet_plan(s):
    n1 = s // DFT_N2
    if n1 <= FNET_KRON_MAX_N1:
        return _FnetPlan(n1, True, BF16_SUBLANES, BF16, max(1, FNET_A_ROWS // (n1 * BF16_SUBLANES)))
    return _FnetPlan(n1, False, SUBLANES, F32, 1)


def _fnet_tables(s):
    n1, kron, tile = _fnet_plan(s)[:3]
    c_ch, s_ch = _dft_cos_sin(C_GROUP_DIM)
    chan = np.concatenate([c_ch, s_ch], axis=1) * C_GROUP_DIM ** -0.5
    c1, s1 = _dft_cos_sin(n1)
    if kron:
        eye = np.eye(tile)
        c1, s1 = np.kron(c1, eye), np.kron(s1, eye)
    m_a = np.block([[c1, -s1], [-s1, -c1]]) * s ** -0.5
    c2, s2 = _dft_cos_sin(DFT_N2)
    m_b = np.concatenate([c2, s2], axis=1)
    ang = (np.outer(np.arange(n1), np.arange(DFT_N2)) % s) * (2.0 * np.pi / s)
    n_groups = DFT_N2 // tile
    if kron:
        lanes = lambda t: np.broadcast_to(t.reshape(n1, n_groups, tile, 1), (n1, n_groups, tile, 128))
    else:
        lanes = lambda t: np.tile(t.reshape(n1, n_groups, tile).transpose(1, 0, 2), (1, 1, 128 // tile))
    as_bf16 = lambda t: jnp.asarray(t, F32).astype(BF16)
    as_f32 = lambda t: jnp.asarray(np.ascontiguousarray(t), F32)
    return as_bf16(chan), as_bf16(m_a), as_bf16(m_b), as_f32(lanes(np.cos(ang))), as_f32(lanes(np.sin(ang)))


def _fnet_a_kernel(x_ref, g_ref, chan_ref, m_ref, tc_ref, ts_ref, ar_ref, ai_ref, *scratch, kron):
    n1, steps, tile = x_ref.shape[:3]
    rows = n1 * tile
    panels = D_MODEL // 128
    widen = lambda t: jnp.concatenate([t] * panels, axis=1)

    def emit(ref, a_i, val):
        ref[:, a_i] = val.astype(ref.dtype).reshape(n1, tile, D_MODEL)

    for a_i in range(steps):
        x = x_ref[:, a_i].reshape(rows, D_MODEL)
        h = _rms(x, g_ref[0:1, :]).astype(BF16)
        zc, zs = [], []
        for g in range(C_GROUPS):
            z = _dot(h[:, g * C_GROUP_DIM:(g + 1) * C_GROUP_DIM], chan_ref[...])
            zc.append(z[:, :C_GROUP_DIM])
            zs.append(z[:, C_GROUP_DIM:])
        if kron:
            z = jnp.concatenate([jnp.concatenate(zc, axis=1), jnp.concatenate(zs, axis=1)], axis=0).astype(BF16)
            a = _dot(m_ref[...], z)
            a_re, a_im = a[:rows, :], a[rows:, :]
            tc = widen(tc_ref[:, a_i].reshape(rows, 128))
            ts = widen(ts_ref[:, a_i].reshape(rows, 128))
            emit(ar_ref, a_i, a_re * tc + a_im * ts)
            emit(ai_ref, a_i, a_im * tc - a_re * ts)
        else:
            z_scr, a_scr = scratch
            for c in range(panels):
                z_scr[c] = zc[c]
                z_scr[panels + c] = zs[c]
            for n2b in range(tile):
                take = lambda p: z_scr[p, pl.ds(n2b, n1, stride=tile), :]
                z = jnp.concatenate([jnp.concatenate([take(c) for c in range(panels)], axis=1),
                                     jnp.concatenate([take(panels + c) for c in range(panels)], axis=1)],
                                    axis=0).astype(BF16)
                a = _dot(m_ref[...], z)
                a_re, a_im = a[:n1, :], a[n1:, :]
                one_lane = lambda ref: widen(jnp.broadcast_to(ref[a_i][:, n2b:n2b + 1], (n1, 128)))
                tc = one_lane(tc_ref)
                ts = one_lane(ts_ref)
                o_re = a_re * tc + a_im * ts
                o_im = a_im * tc - a_re * ts
                for c in range(panels):
                    a_scr[c, pl.ds(n2b, n1, stride=tile), :] = o_re[:, c * 128:(c + 1) * 128]
                    a_scr[panels + c, pl.ds(n2b, n1, stride=tile), :] = o_im[:, c * 128:(c + 1) * 128]
            gather = lambda base: jnp.concatenate([a_scr[base + c] for c in range(panels)], axis=1)
            emit(ar_ref, a_i, gather(0))
            emit(ai_ref, a_i, gather(panels))


def _fnet_b_kernel(x_ref, ar_ref, ai_ref, g_ref, cs_ref, w_ref, o_ref, f_scr):
    rows = DFT_N2 * SUBLANES
    for k1b in range(SUBLANES):
        a = jnp.concatenate([ar_ref[k1b].reshape(DFT_N2, D_MODEL), ai_ref[k1b].reshape(DFT_N2, D_MODEL)],
                            axis=0).astype(BF16)
        f = _dot(cs_ref[...], a)
        for c in range(f_scr.shape[0]):
            f_scr[c, pl.ds(k1b, DFT_N2, stride=SUBLANES), :] = f[:, c * 128:(c + 1) * 128]
    f = jnp.concatenate([f_scr[c] for c in range(f_scr.shape[0])], axis=1).astype(BF16)
    y = _dot(f, w_ref[...])
    x = x_ref[...].reshape(rows, D_MODEL)
    o_ref[...] = (x + _rms(y, g_ref[1:2, :])).reshape(o_ref.shape)


def _fnet_call(x, gains, w_out, tables):
    b, s, d = x.shape
    chan, m_a, m_b, tw_c, tw_s = tables
    n1, kron, tile, a_dtype, nb = _fnet_plan(s)
    n_groups = DFT_N2 // tile
    xa = x.reshape(b, n1, n_groups, tile, d)
    a_spec = pl.BlockSpec((None, n1, nb, tile, d), lambda bi, j: (bi, 0, j, 0, 0))
    if kron:
        tw_spec = pl.BlockSpec((n1, nb, tile, 128), lambda bi, j: (0, j, 0, 0))
        scratch = []
    else:
        tw_spec = pl.BlockSpec((nb, n1, 128), lambda bi, j: (j, 0, 0))
        scratch = [pltpu.VMEM((2 * d // 128, n1 * tile, 128), F32)] * 2
    a_re, a_im = pl.pallas_call(
        functools.partial(_fnet_a_kernel, kron=kron),
        grid=(b, n_groups // nb),
        in_specs=[a_spec, _const_spec(gains), _const_spec(chan), _const_spec(m_a), tw_spec, tw_spec],
        out_specs=[a_spec, a_spec],
        out_shape=[jax.ShapeDtypeStruct(xa.shape, a_dtype)] * 2,
        scratch_shapes=scratch,
        compiler_params=_params(2),
        name="fnet_a",
    )(xa, _raw(gains), chan, m_a, tw_c, tw_s)
    k1a = n1 // SUBLANES
    xb = x.reshape(b, DFT_N2, k1a, SUBLANES, d)
    in_spec = pl.BlockSpec((None, SUBLANES, n_groups, tile, d), lambda bi, j: (bi, j, 0, 0, 0))
    x_spec = pl.BlockSpec((None, DFT_N2, None, SUBLANES, d), lambda bi, j: (bi, 0, j, 0, 0))
    out = pl.pallas_call(
        _fnet_b_kernel,
        grid=(b, k1a),
        in_specs=[x_spec, in_spec, in_spec, _const_spec(gains), _const_spec(m_b), _const_spec(w_out)],
        out_specs=x_spec,
        out_shape=jax.ShapeDtypeStruct(xb.shape, F32),
        scratch_shapes=[pltpu.VMEM((d // 128, DFT_N2 * SUBLANES, 128), F32)],
        compiler_params=_params(2),
        name="fnet_b",
    )(xb, a_re, a_im, _raw(gains), m_b, _raw(w_out))
    return out.reshape(b, s, d)


def _trunk(x, p):
    b, s, d = x.shape
    t = b * s
    for i in range(DEPTH):
        m, j = i % N_MIXERS, i // N_MIXERS
        g = _Layer(p["norm_gains"], i)
        mixer = lambda name: _Layer(p[name], j)
        if m == 0:
            x = _gmlp_call(x.reshape(t, d), g, mixer("a_w_in"), mixer("a_v_norm"), mixer("a_w_s"),
                           mixer("a_b_s_t"), mixer("a_w_out")).reshape(b, s, d)
        elif m == 1:
            cos, sin = _rope_tables(s)
            q, k, v = _qkv_call(x.reshape(t, d), g, mixer("b_w_qkv"), mixer("b_q_norm"), mixer("b_k_norm"),
                                cos, sin)
            kv_w = N_KV_HEADS * HEAD_DIM
            x = _attn_call(x, q.reshape(b, s, d), k.reshape(b, s, kv_w), v.reshape(b, s, kv_w), g, mixer("b_w_o"),
                           **_attn_config(s))
        else:
            x = _fnet_call(x, g, mixer("c_w_out"), _fnet_tables(s))
        x = _mlp_call(x.reshape(t, d), g, _Layer(p["mlp_w_in"], i), _Layer(p["mlp_w_out"], i)).reshape(b, s, d)
    return x


def kernel(x_prompt, x_sample, norm_gains, a_w_in, a_v_norm, a_w_s, a_b_s, a_w_out,
           b_w_qkv, b_q_norm, b_k_norm, b_w_o, c_w_out, mlp_w_in, mlp_w_out):
    b_w_qkv, b_q_norm, b_k_norm = _permute_qk_heads(b_w_qkv, b_q_norm, b_k_norm)
    p = {
        "norm_gains": norm_gains,
        "a_w_in": a_w_in.astype(BF16),
        "a_v_norm": a_v_norm[:, None, :],
        "a_w_s": a_w_s.astype(BF16),
        "a_b_s_t": jnp.swapaxes(a_b_s, 1, 2),
        "a_w_out": a_w_out.astype(BF16),
        "b_w_qkv": b_w_qkv.astype(BF16),
        "b_q_norm": b_q_norm[:, None, :],
        "b_k_norm": b_k_norm[:, None, :],
        "b_w_o": b_w_o.astype(BF16),
        "c_w_out": c_w_out.astype(BF16),
        "mlp_w_in": mlp_w_in.astype(BF16),
        "mlp_w_out": mlp_w_out.astype(BF16),
    }
    return (_trunk(x_prompt, p), _trunk(x_sample, p))
```

```python
import functools
import math
from typing import NamedTuple

import jax
import jax.numpy as jnp
import numpy as np
from jax import lax
from jax.experimental import pallas as pl
from jax.experimental.pallas import tpu as pltpu

D_MODEL = 1024
DEPTH = 4
N_MIXERS = 3
GRID_W = 64
NORM_EPS = 1e-6
CHUNK = 128
A_WIDTH = 3 * D_MODEL
A_GROUPS = 8
A_GROUP_DIM = A_WIDTH // A_GROUPS
HEAD_DIM = 128
N_HEADS = D_MODEL // HEAD_DIM
N_KV_HEADS = 2
Q_PER_KV = N_HEADS // N_KV_HEADS
AXIS_DIM = HEAD_DIM // 2
ROPE_THETA = 10000.0
C_GROUPS = 8
C_GROUP_DIM = D_MODEL // C_GROUPS
D_FF = 4 * D_MODEL

SUBLANES = 8
BF16_SUBLANES = 16
VMEM_LIMIT_BYTES = 56 * 1024 * 1024

DFT_N2 = 128

BF16 = jnp.bfloat16
F32 = jnp.float32


def _rms(x, g):
    return x * lax.rsqrt(jnp.mean(x * x, axis=-1, keepdims=True) + NORM_EPS) * g


def _gelu(x):
    return 0.5 * x * (1.0 + lax.erf(x * math.sqrt(0.5)))


def _dot(a, b):
    return jnp.dot(a, b, preferred_element_type=F32)


class _Layer(NamedTuple):
    stacked: jax.Array
    index: int


def _const_spec(operand):
    if isinstance(operand, _Layer):
        shape = (None,) + operand.stacked.shape[1:]
        index = (operand.index,) + (0,) * (len(shape) - 1)
    else:
        shape, index = operand.shape, (0,) * operand.ndim
    return pl.BlockSpec(shape, lambda *_: index, pipeline_mode=pl.Buffered(1))


def _raw(operand):
    return operand.stacked if isinstance(operand, _Layer) else operand


def _params(n_axes):
    return pltpu.CompilerParams(dimension_semantics=("parallel",) * n_axes,
                                vmem_limit_bytes=VMEM_LIMIT_BYTES)


def _mlp_kernel(x_ref, g_ref, w1_ref, w2_ref, o_ref, *, ff_chunk):
    x = x_ref[...]
    h = _rms(x, g_ref[2:3, :]).astype(BF16)
    acc = jnp.zeros(x.shape, F32)
    for c in range(D_FF // ff_chunk):
        cols = slice(c * ff_chunk, (c + 1) * ff_chunk)
        a = _dot(h, w1_ref[:, cols])
        a = jnp.square(jnp.maximum(a, 0.0)).astype(BF16)
        acc = acc + _dot(a, w2_ref[cols, :])
    o_ref[...] = x + _rms(acc, g_ref[3:4, :])


def _mlp_call(x, gains, w1, w2, *, tm=512, ff_chunk=1024):
    t, d = x.shape
    row_spec = pl.BlockSpec((tm, d), lambda i: (i, 0))
    return pl.pallas_call(
        functools.partial(_mlp_kernel, ff_chunk=ff_chunk),
        grid=(t // tm,),
        in_specs=[row_spec, _const_spec(gains), _const_spec(w1), _const_spec(w2)],
        out_specs=row_spec,
        out_shape=jax.ShapeDtypeStruct(x.shape, x.dtype),
        compiler_params=_params(1),
        name="mlp",
    )(x, _raw(gains), _raw(w1), _raw(w2))


def _gmlp_kernel(x_ref, g_ref, w_in_ref, vn_ref, ws_ref, bs_ref, w_out_ref, o_ref, v_scr):
    x = x_ref[...]
    tm = x.shape[0]
    h = _rms(x, g_ref[0:1, :]).astype(BF16)
    pair = 2 * A_GROUP_DIM
    ss = jnp.zeros((tm, 1), F32)
    for gp in range(A_GROUPS // 2):
        cols = slice(gp * pair, (gp + 1) * pair)
        vg = _gelu(_dot(h, w_in_ref[:, A_WIDTH + gp * pair:A_WIDTH + (gp + 1) * pair]))
        ss = ss + jnp.sum(vg * vg, axis=-1, keepdims=True)
        v_scr[:, cols] = vg
    r = lax.rsqrt(ss * (1.0 / A_WIDTH) + NORM_EPS)
    acc = jnp.zeros(x.shape, F32)
    for gp in range(A_GROUPS // 2):
        cols = slice(gp * pair, (gp + 1) * pair)
        vn = (v_scr[:, cols] * r * vn_ref[:, cols]).astype(BF16)
        u = _gelu(_dot(h, w_in_ref[:, cols]))
        gated = []
        for c in range(tm // CHUNK):
            rows = slice(c * CHUNK, (c + 1) * CHUNK)
            halves = []
            for k in range(2):
                g = 2 * gp + k
                gcols = slice(k * A_GROUP_DIM, (k + 1) * A_GROUP_DIM)
                sv = _dot(ws_ref[g], vn[rows, gcols]) + bs_ref[:, g:g + 1]
                halves.append((u[rows, gcols] * sv).astype(BF16))
            gated.append(jnp.concatenate(halves, axis=1))
        acc = acc + _dot(jnp.concatenate(gated, axis=0), w_out_ref[cols, :])
    o_ref[...] = x + _rms(acc, g_ref[1:2, :])


def _gmlp_call(x, gains, w_in, v_norm, w_s, b_s_t, w_out, *, tm=512):
    t, d = x.shape
    row_spec = pl.BlockSpec((tm, d), lambda i: (i, 0))
    return pl.pallas_call(
        _gmlp_kernel,
        grid=(t // tm,),
        in_specs=[row_spec, _const_spec(gains), _const_spec(w_in), _const_spec(v_norm),
                  _const_spec(w_s), _const_spec(b_s_t), _const_spec(w_out)],
        out_specs=row_spec,
        out_shape=jax.ShapeDtypeStruct(x.shape, x.dtype),
        scratch_shapes=[pltpu.VMEM((tm, A_WIDTH), F32)],
        compiler_params=_params(1),
        name="gmlp",
    )(x, _raw(gains), _raw(w_in), _raw(v_norm), _raw(w_s), _raw(b_s_t), _raw(w_out))


_QUARTER = AXIS_DIM // 2
_HEAD_PERM = tuple(list(range(0, _QUARTER)) + list(range(2 * _QUARTER, 3 * _QUARTER))
                   + list(range(_QUARTER, 2 * _QUARTER)) + list(range(3 * _QUARTER, 4 * _QUARTER)))


def _permute_qk_heads(w_qkv, q_norm, k_norm):
    n_rot = (N_HEADS + N_KV_HEADS) * HEAD_DIM
    perm = jnp.asarray(_HEAD_PERM, jnp.int32)
    cols = (jnp.arange(n_rot, dtype=jnp.int32).reshape(-1, HEAD_DIM)[:, perm]).reshape(-1)
    cols = jnp.concatenate([cols, jnp.arange(n_rot, w_qkv.shape[-1], dtype=jnp.int32)])
    return jnp.take(w_qkv, cols, axis=-1), jnp.take(q_norm, perm, axis=-1), jnp.take(k_norm, perm, axis=-1)


def _rope_tables(s):
    rows = s // GRID_W
    inv_freq = ROPE_THETA ** (-jnp.arange(0, AXIS_DIM, 2, dtype=F32) / AXIS_DIM)
    ang_row = jnp.arange(rows, dtype=F32)[:, None] * inv_freq[None, :]
    ang_col = jnp.arange(GRID_W, dtype=F32)[:, None] * inv_freq[None, :]
    shape = (rows, GRID_W, AXIS_DIM // 2)
    by_row = lambda t: jnp.broadcast_to(t[:, None, :], shape)
    by_col = lambda t: jnp.broadcast_to(t[None, :, :], shape)
    cr, sr, cc, sc = by_row(jnp.cos(ang_row)), by_row(jnp.sin(ang_row)), by_col(jnp.cos(ang_col)), by_col(jnp.sin(ang_col))
    cos = jnp.concatenate([cr, cc, cr, cc], axis=-1).reshape(s, HEAD_DIM)
    sin = jnp.concatenate([-sr, -sc, sr, sc], axis=-1).reshape(s, HEAD_DIM)
    return cos, sin


def _qkv_kernel(x_ref, g_ref, w_ref, qn_ref, kn_ref, cos_ref, sin_ref, qt_ref, k_ref, vt_ref):
    h = _rms(x_ref[...], g_ref[0:1, :]).astype(BF16)
    cos = cos_ref[...]
    sin = sin_ref[...]
    scale = HEAD_DIM ** -0.5 * math.log2(math.e)

    pair = 2 * HEAD_DIM
    blk = lambda axis: lax.broadcasted_iota(jnp.int32, (pair, pair), axis) // HEAD_DIM
    head_ones = (blk(0) == blk(1)).astype(BF16)

    def norm_rope(y, gain):
        sq = y * y
        hi = sq.astype(BF16)
        lo = (sq - hi.astype(F32)).astype(BF16)
        ms = (_dot(hi, head_ones) + _dot(lo, head_ones)) * (1.0 / HEAD_DIM)
        heads = []
        for k in range(2):
            cols = slice(k * HEAD_DIM, (k + 1) * HEAD_DIM)
            t = y[:, cols] * lax.rsqrt(ms[:, cols] + NORM_EPS) * gain
            heads.append(t * cos + pltpu.roll(t, HEAD_DIM // 2, 1) * sin)
        return heads

    n_q_tiles = N_HEADS // 2
    n_k_tiles = N_KV_HEADS // 2
    wide = 2 * pair
    for blk_i in range(w_ref.shape[1] // wide):
        yy = _dot(h, w_ref[:, blk_i * wide:(blk_i + 1) * wide])
        for half in range(2):
            tile = 2 * blk_i + half
            cols = slice(tile * pair, (tile + 1) * pair)
            y = yy[:, half * pair:(half + 1) * pair]
            if tile < n_q_tiles:
                q = jnp.concatenate(norm_rope(y, qn_ref[...]), axis=1) * scale
                qt_ref[cols, :] = q.T.astype(BF16)
            elif tile < n_q_tiles + n_k_tiles:
                kcols = slice((tile - n_q_tiles) * pair, (tile - n_q_tiles + 1) * pair)
                k_ref[:, kcols] = jnp.concatenate(norm_rope(y, kn_ref[...]), axis=1).astype(BF16)
            else:
                vcols = slice((tile - n_q_tiles - n_k_tiles) * pair, (tile - n_q_tiles - n_k_tiles + 1) * pair)
                vt_ref[vcols, :] = y.T.astype(BF16)


def _qkv_call(x, gains, w_qkv, q_norm, k_norm, cos, sin, *, batch, tm):
    t, d = x.shape
    s = cos.shape[0]
    bps = s // tm
    row = lambda i: (i, 0)
    pos = lambda i: (i % bps, 0)
    kv_w = N_KV_HEADS * HEAD_DIM
    return pl.pallas_call(
        _qkv_kernel,
        grid=(t // tm,),
        in_specs=[pl.BlockSpec((tm, d), row), _const_spec(gains), _const_spec(w_qkv),
                  _const_spec(q_norm), _const_spec(k_norm),
                  pl.BlockSpec((tm, HEAD_DIM), pos), pl.BlockSpec((tm, HEAD_DIM), pos)],
        out_specs=[pl.BlockSpec((None, d, tm), lambda i: (i // bps, 0, i % bps)),
                   pl.BlockSpec((tm, kv_w), row),
                   pl.BlockSpec((None, None, kv_w, tm), lambda i: (i // bps, i % bps, 0, 0))],
        out_shape=[jax.ShapeDtypeStruct((batch, d, s), BF16), jax.ShapeDtypeStruct((t, kv_w), BF16),
                   jax.ShapeDtypeStruct((batch, bps, kv_w, tm), BF16)],
        compiler_params=_params(1),
        name="qkv",
    )(x, _raw(gains), _raw(w_qkv), _raw(q_norm), _raw(k_norm), cos, sin)


def _attn_kernel(x_ref, qt_ref, k_ref, vt_ref, g_ref, wo_ref, o_ref, *, unroll, sub_tq, phase_streams):
    n_chunks, _, tk = vt_ref.shape[1:]
    cols = Q_PER_KV * sub_tq
    ones = jnp.ones((BF16_SUBLANES, tk), BF16)
    head_rows = lambda hd: slice(hd * HEAD_DIM, (hd + 1) * HEAD_DIM)
    streams = [(g, slice(r * sub_tq, (r + 1) * sub_tq))
               for g in range(N_KV_HEADS) for r in range(qt_ref.shape[2] // sub_tq)]
    qts = [jnp.concatenate([qt_ref[0, head_rows(g * Q_PER_KV + hd), qs] for hd in range(Q_PER_KV)], axis=1)
           for g, qs in streams]

    def run(phase):
        def scores(j):
            start = pl.multiple_of(j * tk, tk)
            return tuple(_dot(k_ref[0, pl.ds(start, tk), head_rows(streams[i][0])], qts[i]) for i in phase)

        def accumulate(j, scs, carry):
            out = []
            for i, sc, (m, acc) in zip(phase, scs, carry):
                vt = jnp.concatenate([vt_ref[0, j, head_rows(streams[i][0]), :], ones], axis=0)
                m_new = jnp.maximum(m, jnp.max(sc, axis=0, keepdims=True))
                alpha = jnp.exp2(m - m_new)
                p = jnp.exp2(sc - m_new)
                out.append((m_new, alpha * acc + _dot(vt, p.astype(BF16))))
            return tuple(out)

        init = (jnp.full((1, cols), -jnp.inf, F32), jnp.zeros((HEAD_DIM + BF16_SUBLANES, cols), F32))
        carry, scs = (init,) * len(phase), scores(0)
        if unroll >= n_chunks:
            for j in range(n_chunks):
                nxt = scores(j + 1) if j + 1 < n_chunks else None
                carry, scs = accumulate(j, scs, carry), nxt
            return carry

        def body(j, state):
            carry, scs = state
            nxt = scores(jnp.minimum(j + 1, n_chunks - 1))
            return accumulate(j, scs, carry), nxt

        return lax.fori_loop(0, n_chunks, body, (carry, scs), unroll=unroll)[0]

    final = []
    for first in range(0, len(streams), phase_streams):
        final.extend(run(range(first, first + phase_streams)))
    heads = {}
    for (g, qs), (_, acc) in zip(streams, final):
        o_t = acc[:HEAD_DIM, :] / acc[HEAD_DIM:HEAD_DIM + 1, :]
        for hd in range(Q_PER_KV):
            heads[(qs.start, g * Q_PER_KV + hd)] = o_t[:, hd * sub_tq:(hd + 1) * sub_tq].T.astype(BF16)
    for r0 in sorted({r for r, _ in heads}):
        a = jnp.concatenate([heads[(r0, hd)] for hd in range(N_HEADS)], axis=1)
        y = _dot(a, wo_ref[...])
        rows_r = slice(r0, r0 + sub_tq)
        o_ref[0, rows_r, :] = x_ref[0, rows_r, :] + _rms(y, g_ref[1:2, :])


ATTN_SUB_TQ = 256
ATTN_TK = 512
ATTN_BODY_TRIPS = 16


def _attn_config(s):
    trips = s // ATTN_TK
    phase_streams = max(1, ATTN_BODY_TRIPS // trips)
    q_blocks = max(1, phase_streams // N_KV_HEADS)
    phase_streams = min(phase_streams, N_KV_HEADS * q_blocks)
    return dict(tq=q_blocks * ATTN_SUB_TQ, sub_tq=ATTN_SUB_TQ, phase_streams=phase_streams,
                unroll=min(trips, ATTN_BODY_TRIPS // phase_streams))


def _attn_call(x, qt, k, vt, gains, w_o, *, tq, sub_tq, unroll, phase_streams):
    b, s, d = x.shape
    row_spec = pl.BlockSpec((1, tq, d), lambda bi, i: (bi, i, 0))
    qt_spec = pl.BlockSpec((1, d, tq), lambda bi, i: (bi, 0, i))
    k_spec = pl.BlockSpec((1,) + k.shape[1:], lambda bi, i: (bi, 0, 0), pipeline_mode=pl.Buffered(1))
    vt_spec = pl.BlockSpec((1,) + vt.shape[1:], lambda bi, i: (bi, 0, 0, 0), pipeline_mode=pl.Buffered(1))
    return pl.pallas_call(
        functools.partial(_attn_kernel, unroll=unroll, sub_tq=sub_tq, phase_streams=phase_streams),
        grid=(b, s // tq),
        in_specs=[row_spec, qt_spec, k_spec, vt_spec, _const_spec(gains), _const_spec(w_o)],
        out_specs=row_spec,
        out_shape=jax.ShapeDtypeStruct(x.shape, x.dtype),
        compiler_params=_params(2),
        name="attn",
    )(x, qt, k, vt, _raw(gains), _raw(w_o))


FNET_KRON_MAX_N1 = 16
FNET_A_ROWS = 1024


class _FnetPlan(NamedTuple):
    n1: int
    kron: bool
    tile: int
    a_dtype: object
    groups_per_step: int


def _dft_cos_sin(n):
    idx = np.arange(n)
    ang = (np.outer(idx, idx) % n) * (2.0 * np.pi / n)
    return np.cos(ang), np.sin(ang)


def _fnet_plan(s):
    n1 = s // DFT_N2
    if n1 <= FNET_KRON_MAX_N1:
        return _FnetPlan(n1, True, BF16_SUBLANES, BF16, max(1, FNET_A_ROWS // (n1 * BF16_SUBLANES)))
    return _FnetPlan(n1, False, SUBLANES, F32, 1)


def _fnet_tables(s):
    n1, kron, tile = _fnet_plan(s)[:3]
    c_ch, s_ch = _dft_cos_sin(C_GROUP_DIM)
    chan = np.concatenate([c_ch, s_ch], axis=1) * C_GROUP_DIM ** -0.5
    c1, s1 = _dft_cos_sin(n1)
    if kron:
        eye = np.eye(tile)
        c1, s1 = np.kron(c1, eye), np.kron(s1, eye)
    m_a = np.block([[c1, -s1], [-s1, -c1]]) * s ** -0.5
    c2, s2 = _dft_cos_sin(DFT_N2)
    m_b = np.concatenate([c2, s2], axis=1)
    ang = (np.outer(np.arange(n1), np.arange(DFT_N2)) % s) * (2.0 * np.pi / s)
    n_groups = DFT_N2 // tile
    if kron:
        lanes = lambda t: np.broadcast_to(t.reshape(n1, n_groups, tile, 1), (n1, n_groups, tile, 128))
    else:
        lanes = lambda t: np.tile(t.reshape(n1, n_groups, tile).transpose(1, 0, 2), (1, 1, 128 // tile))
    as_bf16 = lambda t: jnp.asarray(t, F32).astype(BF16)
    as_f32 = lambda t: jnp.asarray(np.ascontiguousarray(t), F32)
    return as_bf16(chan), as_bf16(m_a), as_bf16(m_b), as_f32(lanes(np.cos(ang))), as_f32(lanes(np.sin(ang)))


def _fnet_a_kernel(x_ref, g_ref, chan_ref, m_ref, tc_ref, ts_ref, ar_ref, ai_ref, *scratch, kron):
    n1, steps, tile = x_ref.shape[:3]
    rows = n1 * tile
    panels = D_MODEL // 128
    widen = lambda t: jnp.concatenate([t] * panels, axis=1)

    def emit(ref, a_i, val):
        ref[:, a_i] = val.astype(ref.dtype).reshape(n1, tile, D_MODEL)

    for a_i in range(steps):
        x = x_ref[:, a_i].reshape(rows, D_MODEL)
        h = _rms(x, g_ref[0:1, :]).astype(BF16)
        zc, zs = [], []
        for g in range(C_GROUPS):
            z = _dot(h[:, g * C_GROUP_DIM:(g + 1) * C_GROUP_DIM], chan_ref[...])
            zc.append(z[:, :C_GROUP_DIM])
            zs.append(z[:, C_GROUP_DIM:])
        if kron:
            z = jnp.concatenate([jnp.concatenate(zc, axis=1), jnp.concatenate(zs, axis=1)], axis=0).astype(BF16)
            a = _dot(m_ref[...], z)
            a_re, a_im = a[:rows, :], a[rows:, :]
            tc = widen(tc_ref[:, a_i].reshape(rows, 128))
            ts = widen(ts_ref[:, a_i].reshape(rows, 128))
            emit(ar_ref, a_i, a_re * tc + a_im * ts)
            emit(ai_ref, a_i, a_im * tc - a_re * ts)
        else:
            z_scr, a_scr = scratch
            for c in range(panels):
                z_scr[c] = zc[c]
                z_scr[panels + c] = zs[c]
            for n2b in range(tile):
                take = lambda p: z_scr[p, pl.ds(n2b, n1, stride=tile), :]
                z = jnp.concatenate([jnp.concatenate([take(c) for c in range(panels)], axis=1),
                                     jnp.concatenate([take(panels + c) for c in range(panels)], axis=1)],
                                    axis=0).astype(BF16)
                a = _dot(m_ref[...], z)
                a_re, a_im = a[:n1, :], a[n1:, :]
                one_lane = lambda ref: widen(jnp.broadcast_to(ref[a_i][:, n2b:n2b + 1], (n1, 128)))
                tc = one_lane(tc_ref)
                ts = one_lane(ts_ref)
                o_re = a_re * tc + a_im * ts
                o_im = a_im * tc - a_re * ts
                for c in range(panels):
                    a_scr[c, pl.ds(n2b, n1, stride=tile), :] = o_re[:, c * 128:(c + 1) * 128]
                    a_scr[panels + c, pl.ds(n2b, n1, stride=tile), :] = o_im[:, c * 128:(c + 1) * 128]
            gather = lambda base: jnp.concatenate([a_scr[base + c] for c in range(panels)], axis=1)
            emit(ar_ref, a_i, gather(0))
            emit(ai_ref, a_i, gather(panels))


def _fnet_b_kernel(x_ref, ar_ref, ai_ref, g_ref, cs_ref, w_ref, o_ref, f_scr):
    rows = DFT_N2 * SUBLANES
    for k1b in range(SUBLANES):
        a = jnp.concatenate([ar_ref[k1b].reshape(DFT_N2, D_MODEL), ai_ref[k1b].reshape(DFT_N2, D_MODEL)],
                            axis=0).astype(BF16)
        f = _dot(cs_ref[...], a)
        for c in range(f_scr.shape[0]):
            f_scr[c, pl.ds(k1b, DFT_N2, stride=SUBLANES), :] = f[:, c * 128:(c + 1) * 128]
    f = jnp.concatenate([f_scr[c] for c in range(f_scr.shape[0])], axis=1).astype(BF16)
    y = _dot(f, w_ref[...])
    x = x_ref[...].reshape(rows, D_MODEL)
    o_ref[...] = (x + _rms(y, g_ref[1:2, :])).reshape(o_ref.shape)


def _fnet_call(x, gains, w_out, tables):
    b, s, d = x.shape
    chan, m_a, m_b, tw_c, tw_s = tables
    n1, kron, tile, a_dtype, nb = _fnet_plan(s)
    n_groups = DFT_N2 // tile
    xa = x.reshape(b, n1, n_groups, tile, d)
    a_spec = pl.BlockSpec((None, n1, nb, tile, d), lambda bi, j: (bi, 0, j, 0, 0))
    if kron:
        tw_spec = pl.BlockSpec((n1, nb, tile, 128), lambda bi, j: (0, j, 0, 0))
        scratch = []
    else:
        tw_spec = pl.BlockSpec((nb, n1, 128), lambda bi, j: (j, 0, 0))
        scratch = [pltpu.VMEM((2 * d // 128, n1 * tile, 128), F32)] * 2
    a_re, a_im = pl.pallas_call(
        functools.partial(_fnet_a_kernel, kron=kron),
        grid=(b, n_groups // nb),
        in_specs=[a_spec, _const_spec(gains), _const_spec(chan), _const_spec(m_a), tw_spec, tw_spec],
        out_specs=[a_spec, a_spec],
        out_shape=[jax.ShapeDtypeStruct(xa.shape, a_dtype)] * 2,
        scratch_shapes=scratch,
        compiler_params=_params(2),
        name="fnet_a",
    )(xa, _raw(gains), chan, m_a, tw_c, tw_s)
    k1a = n1 // SUBLANES
    xb = x.reshape(b, DFT_N2, k1a, SUBLANES, d)
    in_spec = pl.BlockSpec((None, SUBLANES, n_groups, tile, d), lambda bi, j: (bi, j, 0, 0, 0))
    x_spec = pl.BlockSpec((None, DFT_N2, None, SUBLANES, d), lambda bi, j: (bi, 0, j, 0, 0))
    out = pl.pallas_call(
        _fnet_b_kernel,
        grid=(b, k1a),
        in_specs=[x_spec, in_spec, in_spec, _const_spec(gains), _const_spec(m_b), _const_spec(w_out)],
        out_specs=x_spec,
        out_shape=jax.ShapeDtypeStruct(xb.shape, F32),
        scratch_shapes=[pltpu.VMEM((d // 128, DFT_N2 * SUBLANES, 128), F32)],
        compiler_params=_params(2),
        name="fnet_b",
    )(xb, a_re, a_im, _raw(gains), m_b, _raw(w_out))
    return out.reshape(b, s, d)


def _trunk(x, p):
    b, s, d = x.shape
    t = b * s
    for i in range(DEPTH):
        m, j = i % N_MIXERS, i // N_MIXERS
        g = _Layer(p["norm_gains"], i)
        mixer = lambda name: _Layer(p[name], j)
        if m == 0:
            x = _gmlp_call(x.reshape(t, d), g, mixer("a_w_in"), mixer("a_v_norm"), mixer("a_w_s"),
                           mixer("a_b_s_t"), mixer("a_w_out")).reshape(b, s, d)
        elif m == 1:
            cos, sin = _rope_tables(s)
            qt, k, vt = _qkv_call(x.reshape(t, d), g, mixer("b_w_qkv"), mixer("b_q_norm"), mixer("b_k_norm"),
                                  cos, sin, batch=b, tm=ATTN_TK)
            x = _attn_call(x, qt, k.reshape(b, s, N_KV_HEADS * HEAD_DIM), vt, g, mixer("b_w_o"), **_attn_config(s))
        else:
            x = _fnet_call(x, g, mixer("c_w_out"), _fnet_tables(s))
        x = _mlp_call(x.reshape(t, d), g, _Layer(p["mlp_w_in"], i), _Layer(p["mlp_w_out"], i)).reshape(b, s, d)
    return x


def kernel(x_prompt, x_sample, norm_gains, a_w_in, a_v_norm, a_w_s, a_b_s, a_w_out,
           b_w_qkv, b_q_norm, b_k_norm, b_w_o, c_w_out, mlp_w_in, mlp_w_out):
    b_w_qkv, b_q_norm, b_k_norm = _permute_qk_heads(b_w_qkv, b_q_norm, b_k_norm)
    p = {
        "norm_gains": norm_gains,
        "a_w_in": a_w_in.astype(BF16),
        "a_v_norm": a_v_norm[:, None, :],
        "a_w_s": a_w_s.astype(BF16),
        "a_b_s_t": jnp.swapaxes(a_b_s, 1, 2),
        "a_w_out": a_w_out.astype(BF16),
        "b_w_qkv": b_w_qkv.astype(BF16),
        "b_q_norm": b_q_norm[:, None, :],
        "b_k_norm": b_k_norm[:, None, :],
        "b_w_o": b_w_o.astype(BF16),
        "c_w_out": c_w_out.astype(BF16),
        "mlp_w_in": mlp_w_in.astype(BF16),
        "mlp_w_out": mlp_w_out.astype(BF16),
    }
    return (_trunk(x_prompt, p), _trunk(x_sample, p))
```
